```python
import math
import jax
import jax.numpy as jnp
from jax import lax
import numpy as np

D_MODEL = 2048
BATCH = 8
SEQ = 2048
DEPTH = 2
DEC_BATCH = 128
DEC_SEQ = 4
PAST_LEN = 16384
PAGE_SIZE = 128

HEAD_DIM = 128
N_EVEN = (DEPTH + 1) // 2
N_ODD = DEPTH // 2
MOBA_HEADS = D_MODEL // (2 * HEAD_DIM)
MOBA_KV_HEADS = 2
MOBA_BLOCK = 256
MOBA_TOPK = 3
MOBA_Q_CHUNK = 16
MLA_HEADS = D_MODEL // (2 * HEAD_DIM)
MLA_Q_LORA = D_MODEL // 4
MLA_KV_LORA = D_MODEL // 8
MLA_NOPE = 128
MLA_ROPE = 64
MLA_V = 128
MLA_QK = MLA_NOPE + MLA_ROPE
ROPE_THETA = 10000.0
FOX_HEADS = D_MODEL // HEAD_DIM
FOX_KV_HEADS = 2
RPE_BUCKETS = 32
RPE_MAX_DIST = 128
D_FF = ((8 * D_MODEL + 3 * 256 - 1) // (3 * 256)) * 256

Q_BLOCK = 128
EPS = 1e-6
NEG_INF = -1e30
ATTN_SCALE = HEAD_DIM ** -0.5
MLA_SCALE = MLA_QK ** -0.5

IN_EVEN_SPLITS = (MOBA_HEADS * HEAD_DIM, MOBA_KV_HEADS * HEAD_DIM, MOBA_KV_HEADS * HEAD_DIM,
                  MLA_Q_LORA, MLA_KV_LORA, MLA_ROPE)
IN_EVEN = sum(IN_EVEN_SPLITS)
MIX_EVEN = MOBA_HEADS * HEAD_DIM + MLA_HEADS * MLA_V
IN_ODD_SPLITS = (FOX_HEADS * HEAD_DIM, FOX_KV_HEADS * HEAD_DIM, FOX_KV_HEADS * HEAD_DIM, FOX_HEADS)
IN_ODD = sum(IN_ODD_SPLITS)
MIX_ODD = FOX_HEADS * HEAD_DIM

kernel_name = "hybrid_moba_mla_fox_decoder_step"

f32 = jnp.float32


def rms_norm(x, g):
    xf = x.astype(f32)
    y = xf * lax.rsqrt(jnp.mean(xf * xf, axis=-1, keepdims=True) + EPS)
    return (y * g.astype(f32)).astype(x.dtype)


def split_cols(z, widths):
    return jnp.split(z, [int(i) for i in np.cumsum(widths)[:-1]], axis=-1)


def swiglu(h, w_gu, w_down):
    g, u = jnp.split(h @ w_gu, 2, axis=-1)
    return (jax.nn.silu(g) * u) @ w_down


def t5_bucket(dist):
    n = jnp.maximum(dist, 0)
    exact = RPE_BUCKETS // 2
    nf = jnp.maximum(n, 1).astype(f32)
    log_b = exact + (jnp.log(nf / exact) / math.log(RPE_MAX_DIST / exact)
                     * (RPE_BUCKETS - exact)).astype(jnp.int32)
    return jnp.where(n < exact, n, jnp.minimum(log_b, RPE_BUCKETS - 1))


def rope_angles(pos):
    inv = ROPE_THETA ** (-jnp.arange(0, MLA_ROPE, 2, dtype=f32) / MLA_ROPE)
    ang = pos.astype(f32)[:, None] * inv[None, :]
    return jnp.cos(ang), jnp.sin(ang)


def apply_rope(x, cos, sin):
    half = x.shape[-1] // 2
    x1, x2 = x[..., :half], x[..., half:]
    return jnp.concatenate([x1 * cos - x2 * sin, x1 * sin + x2 * cos], axis=-1)


def mla_qk_gain(g):
    return jnp.concatenate([g[:MLA_NOPE], g[MLA_NOPE:], g[MLA_NOPE:]])


def gqa_partial(q, k, v, bias, mask, scale):
    n, nq, h, dk = q.shape
    kh = k.shape[2]
    g = h // kh
    s = jnp.einsum('nqkgd,nlkd->nkgql', q.reshape(n, nq, kh, g, dk), k).astype(f32) * scale
    if bias is not None:
        s = s + bias.reshape(bias.shape[0], kh, g, nq, bias.shape[-1]).astype(f32)
    s = jnp.where(mask[:, None, None], s, NEG_INF)
    m = jnp.max(s, axis=-1)
    p = jnp.exp(s - m[..., None])
    l = jnp.sum(p, axis=-1)
    o = jnp.einsum('nkgql,nlkd->nqkgd', p, v.astype(f32)).reshape(n, nq, h, -1)
    return (m.transpose(0, 3, 1, 2).reshape(n, nq, h),
            l.transpose(0, 3, 1, 2).reshape(n, nq, h), o)


def lead(part):
    return tuple(a[None] for a in part)


def merge_partials(*parts):
    m = jnp.concatenate([p[0] for p in parts], axis=0)
    l = jnp.concatenate([p[1] for p in parts], axis=0)
    o = jnp.concatenate([p[2] for p in parts], axis=0)
    mx = jnp.max(m, axis=0)
    w = jnp.exp(m - mx)
    return jnp.sum(o * w[..., None], axis=0) / jnp.sum(l * w, axis=0)[..., None]


def moba_select(q, q_pos, k_mean):
    n, nq, h, d = q.shape
    nb, kh = k_mean.shape[1], k_mean.shape[2]
    own = q_pos // MOBA_BLOCK
    score = jnp.einsum('nqkgd,njkd->nqkgj', q.reshape(n, nq, kh, h // kh, d).astype(f32),
                       k_mean).reshape(n, nq, h, nb)
    past = jnp.arange(nb)[None, :] < own[:, None]
    score = jnp.where(past[None, :, None, :], score, NEG_INF)
    _, sel = lax.top_k(score, min(MOBA_TOPK, nb))
    return sel, sel < own[None, :, None, None]


def moba_selected_partial(q, q_pos, sel, valid, kg, vg, rpe_table):
    n, nq, h, d = q.shape
    s = jnp.einsum('nqhd,nqhjld->nqhjl', q, kg).astype(f32) * ATTN_SCALE
    k_pos = sel[..., None] * MOBA_BLOCK + jnp.arange(MOBA_BLOCK)
    bias = rpe_table[t5_bucket(q_pos[None, :, None, None, None] - k_pos),
                     jnp.arange(h)[:, None, None]].astype(f32)
    s = jnp.where(valid[..., None], s + bias, NEG_INF).reshape(n, nq, h, -1)
    m = jnp.max(s, axis=-1)
    p = jnp.exp(s - m[..., None])
    o = jnp.einsum('nqhl,nqhld->nqhd', p, vg.reshape(n, nq, h, -1, d).astype(f32))
    return m, jnp.sum(p, axis=-1), o


def moba_prompt(q, k, v, rpe_table):
    n, t, h, d = q.shape
    kh = k.shape[2]
    nb = -(-t // MOBA_BLOCK)
    pad = ((0, 0), (0, nb * MOBA_BLOCK - t), (0, 0), (0, 0))
    blk = lambda a: jnp.pad(a, pad).reshape(n * nb, MOBA_BLOCK, a.shape[2], d)
    i = jnp.arange(MOBA_BLOCK)
    dist = i[:, None] - i[None, :]
    bias = rpe_table[t5_bucket(dist)].transpose(2, 0, 1)[None]
    m, l, o = gqa_partial(blk(q), blk(k), blk(v), bias, (dist >= 0)[None], ATTN_SCALE)
    unblk = lambda a: a.reshape((n, nb * MOBA_BLOCK) + a.shape[2:])[:, :t]
    own = (unblk(m)[None], unblk(l)[None], unblk(o)[None])
    nbf = t // MOBA_BLOCK
    if nbf == 0:
        return own[2][0] / own[1][0][..., None]
    kb = k[:, :nbf * MOBA_BLOCK].reshape(n, nbf, MOBA_BLOCK, kh, d)
    vb = v[:, :nbf * MOBA_BLOCK].reshape(n, nbf, MOBA_BLOCK, kh, d)
    k_mean = jnp.mean(kb.astype(f32), axis=2)
    kb_t = kb.transpose(0, 3, 1, 2, 4)
    vb_t = vb.transpose(0, 3, 1, 2, 4)
    n_idx = jnp.arange(n)[:, None, None, None]
    h_idx = (jnp.arange(h) // (h // kh))[None, None, :, None]

    def chunk(args):
        q_c, pos_c = args
        sel, valid = moba_select(q_c, pos_c, k_mean)
        return moba_selected_partial(q_c, pos_c, sel, valid, kb_t[n_idx, h_idx, sel],
                                     vb_t[n_idx, h_idx, sel], rpe_table)

    nc = t // MOBA_Q_CHUNK
    m2, l2, o2 = lax.map(chunk, (q.reshape(n, nc, MOBA_Q_CHUNK, h, d).swapaxes(0, 1),
                                 jnp.arange(t).reshape(nc, MOBA_Q_CHUNK)))
    back = lambda a: a.swapaxes(0, 1).reshape((n, t) + a.shape[3:])[None]
    return merge_partials(own, (back(m2), back(l2), back(o2)))


def moba_sample(q, k_new, v_new, cache_kv, li, page_table, rpe_table):
    n, s, h, d = q.shape
    kh = k_new.shape[2]
    ppb = MOBA_BLOCK // PAGE_SIZE
    n_pages = page_table.shape[1]
    past = n_pages * PAGE_SIZE
    nbf = past // MOBA_BLOCK
    q_pos = past + jnp.arange(s)
    n_own = n_pages - nbf * ppb
    kv_own = cache_kv[li, page_table[:, nbf * ppb:]].reshape(n, n_own * PAGE_SIZE, 2, kh, d)
    k_own = jnp.concatenate([kv_own[:, :, 0], k_new], axis=1)
    v_own = jnp.concatenate([kv_own[:, :, 1], v_new], axis=1)
    dist = q_pos[:, None] - (nbf * MOBA_BLOCK + jnp.arange(k_own.shape[1]))[None, :]
    bias = rpe_table[t5_bucket(dist)].transpose(2, 0, 1)[None]
    own = lead(gqa_partial(q, k_own, v_own, bias, (dist >= 0)[None], ATTN_SCALE))
    if nbf == 0:
        return own[2][0] / own[1][0][..., None]
    blocks = page_table[:, :nbf * ppb].reshape(n, nbf, ppb).swapaxes(0, 1)
    k_mean = lax.map(lambda pg: jnp.mean(cache_kv[li, pg][:, :, :, 0].astype(f32), axis=(1, 2)),
                     blocks).swapaxes(0, 1)
    kv_head = jnp.arange(h) // (h // kh)
    n_idx = jnp.arange(n)[:, None, None, None, None]

    def one_query(args):
        q_i, pos_i = args
        sel, valid = moba_select(q_i, pos_i, k_mean)
        ns = sel.shape[-1]
        pages = page_table[n_idx, sel[..., None] * ppb + jnp.arange(ppb)]
        rows = cache_kv[li, pages[..., None, None], jnp.arange(PAGE_SIZE)[:, None], jnp.arange(2),
                        kv_head[:, None, None, None, None]]
        kg = rows[..., 0, :].reshape(n, 1, h, ns, MOBA_BLOCK, d)
        vg = rows[..., 1, :].reshape(n, 1, h, ns, MOBA_BLOCK, d)
        return moba_selected_partial(q_i, pos_i, sel, valid, kg, vg, rpe_table)

    m, l, o = lax.map(one_query, (q.swapaxes(0, 1)[:, :, None], q_pos[:, None]))
    sel_part = (m[:, :, 0].swapaxes(0, 1)[None], l[:, :, 0].swapaxes(0, 1)[None],
                o[:, :, 0].swapaxes(0, 1)[None])
    return merge_partials(own, sel_part)


def mla_keys_values(rows, w_ukv, k_g):
    n, l, _ = rows.shape
    kv = (rows[..., :MLA_KV_LORA] @ w_ukv).reshape(n, l, MLA_HEADS, MLA_NOPE + MLA_V)
    kpe = jnp.broadcast_to(rows[:, :, None, MLA_KV_LORA:], (n, l, MLA_HEADS, MLA_ROPE)).astype(kv.dtype)
    k = rms_norm(jnp.concatenate([kv[..., :MLA_NOPE], kpe], axis=-1), mla_qk_gain(k_g))
    return k, kv[..., MLA_NOPE:]


def mla_prompt(q, rows, w_ukv, k_g):
    n, t, h, _ = q.shape
    k, v = mla_keys_values(rows, w_ukv, k_g)
    nq = t // Q_BLOCK

    def blk(args):
        q_i, pos_i = args
        mask = (pos_i[:, None] >= jnp.arange(t)[None, :])[None]
        m, l, o = gqa_partial(q_i, k, v, None, mask, MLA_SCALE)
        return o / l[..., None]

    o = lax.map(blk, (q.reshape(n, nq, Q_BLOCK, h, MLA_QK).swapaxes(0, 1),
                      jnp.arange(t).reshape(nq, Q_BLOCK)))
    return o.swapaxes(0, 1).reshape(n, t, h, MLA_V)


def mla_sample(q, rows_new, cache_mla, li, page_table, w_ukv, k_g):
    s = q.shape[1]
    full = jnp.ones((1, s, PAGE_SIZE), bool)

    def page(pg):
        k, v = mla_keys_values(cache_mla[li, pg], w_ukv, k_g)
        return gqa_partial(q, k, v, None, full, MLA_SCALE)

    past = lax.map(page, page_table.T)
    k, v = mla_keys_values(rows_new, w_ukv, k_g)
    causal = (jnp.arange(s)[:, None] >= jnp.arange(s)[None, :])[None]
    return merge_partials(past, lead(gqa_partial(q, k, v, None, causal, MLA_SCALE)))


def fox_prompt(q, k, v, logf):
    n, t, h, d = q.shape
    F = jnp.cumsum(logf, axis=1)
    Ft = F.transpose(0, 2, 1)
    nq = t // Q_BLOCK

    def blk(args):
        q_i, f_i, pos_i = args
        bias = f_i.transpose(0, 2, 1)[..., :, None] - Ft[:, :, None, :]
        mask = (pos_i[:, None] >= jnp.arange(t)[None, :])[None]
        m, l, o = gqa_partial(q_i, k, v, bias, mask, ATTN_SCALE)
        return o / l[..., None]

    o = lax.map(blk, (q.reshape(n, nq, Q_BLOCK, h, d).swapaxes(0, 1),
                      F.reshape(n, nq, Q_BLOCK, h).swapaxes(0, 1),
                      jnp.arange(t).reshape(nq, Q_BLOCK)))
    return o.swapaxes(0, 1).reshape(n, t, h, d)


def fox_sample(q, k_new, v_new, logf_new, cache_kv, cache_logf, li, page_table):
    n, s, h, d = q.shape
    n_pages = page_table.shape[1]
    logf_past = cache_logf[li, page_table].reshape(n, n_pages * PAGE_SIZE, h).astype(f32)
    after = lax.cumsum(logf_past, axis=1, reverse=True) - logf_past
    fn_t = jnp.cumsum(logf_new, axis=1).transpose(0, 2, 1)
    after_pages = after.reshape(n, n_pages, PAGE_SIZE, h).swapaxes(0, 1)
    full = jnp.ones((1, s, PAGE_SIZE), bool)

    def page(args):
        pg, a = args
        kv = cache_kv[li, pg]
        bias = fn_t[..., :, None] + a.transpose(0, 2, 1)[:, :, None, :]
        return gqa_partial(q, kv[:, :, 0], kv[:, :, 1], bias, full, ATTN_SCALE)

    past = lax.map(page, (page_table.T, after_pages))
    bias_new = fn_t[..., :, None] - fn_t[..., None, :]
    causal = (jnp.arange(s)[:, None] >= jnp.arange(s)[None, :])[None]
    return merge_partials(past, lead(gqa_partial(q, k_new, v_new, bias_new, causal, ATTN_SCALE)))


def even_projections(h, pos, ei, prm):
    n, t, _ = h.shape
    qa, ka, va, cq, ckv, kpe = split_cols(h @ prm["w_in_even"][ei], IN_EVEN_SPLITS)
    q_a = rms_norm(qa.reshape(n, t, MOBA_HEADS, HEAD_DIM), prm["moba_q_g"][ei])
    k_a = rms_norm(ka.reshape(n, t, MOBA_KV_HEADS, HEAD_DIM), prm["moba_k_g"][ei])
    v_a = va.reshape(n, t, MOBA_KV_HEADS, HEAD_DIM)
    cos, sin = rope_angles(pos)
    qb = (rms_norm(cq, prm["mla_q_a_g"][ei]) @ prm["w_uq"][ei]).reshape(n, t, MLA_HEADS, MLA_QK)
    qb = jnp.concatenate([qb[..., :MLA_NOPE].astype(f32),
                          apply_rope(qb[..., MLA_NOPE:], cos[:, None], sin[:, None])], axis=-1)
    qb = rms_norm(qb, mla_qk_gain(prm["mla_q_g"][ei]))
    rows = jnp.concatenate([rms_norm(ckv, prm["mla_kv_a_g"][ei]).astype(f32),
                            apply_rope(kpe, cos, sin)], axis=-1)
    return q_a, jnp.stack([k_a, v_a], axis=2), qb, rows


def even_mixer_prompt(h, ei, prm):
    n, t, _ = h.shape
    q_a, kv_a, q_b, rows = even_projections(h, jnp.arange(t), ei, prm)
    o_a = moba_prompt(q_a, kv_a[:, :, 0], kv_a[:, :, 1], prm["rpe_table"])
    o_b = mla_prompt(q_b, rows, prm["w_ukv"][ei], prm["mla_k_g"][ei])
    o = jnp.concatenate([o_a.reshape(n, t, -1), o_b.reshape(n, t, -1)], axis=-1).astype(h.dtype)
    return o @ prm["w_out_even"][ei], (kv_a, rows)


def even_mixer_sample(h, ei, prm, cache_moba_kv, cache_mla, page_table):
    n, s, _ = h.shape
    past = page_table.shape[1] * PAGE_SIZE
    q_a, kv_a, q_b, rows = even_projections(h, past + jnp.arange(s), ei, prm)
    o_a = moba_sample(q_a, kv_a[:, :, 0], kv_a[:, :, 1], cache_moba_kv, ei, page_table, prm["rpe_table"])
    o_b = mla_sample(q_b, rows, cache_mla, ei, page_table, prm["w_ukv"][ei], prm["mla_k_g"][ei])
    o = jnp.concatenate([o_a.reshape(n, s, -1), o_b.reshape(n, s, -1)], axis=-1).astype(h.dtype)
    return o @ prm["w_out_even"][ei], (kv_a, rows)


def odd_projections(h, oi, prm):
    n, t, _ = h.shape
    qc, kc, vc, fc = split_cols(h @ prm["w_in_odd"][oi], IN_ODD_SPLITS)
    q = rms_norm(qc.reshape(n, t, FOX_HEADS, HEAD_DIM), prm["fox_q_g"][oi])
    k = rms_norm(kc.reshape(n, t, FOX_KV_HEADS, HEAD_DIM), prm["fox_k_g"][oi])
    v = vc.reshape(n, t, FOX_KV_HEADS, HEAD_DIM)
    logf = jax.nn.log_sigmoid(fc.astype(f32) + prm["fox_f_b"][oi].astype(f32))
    return q, jnp.stack([k, v], axis=2), logf


def odd_mixer_prompt(h, oi, prm):
    n, t, _ = h.shape
    q, kv, logf = odd_projections(h, oi, prm)
    o = fox_prompt(q, kv[:, :, 0], kv[:, :, 1], logf)
    return o.reshape(n, t, -1).astype(h.dtype) @ prm["w_out_odd"][oi], (kv, logf)


def odd_mixer_sample(h, oi, prm, cache_fox_kv, cache_fox_logf, page_table):
    n, s, _ = h.shape
    q, kv, logf = odd_projections(h, oi, prm)
    o = fox_sample(q, kv[:, :, 0], kv[:, :, 1], logf, cache_fox_kv, cache_fox_logf, oi, page_table)
    return o.reshape(n, s, -1).astype(h.dtype) @ prm["w_out_odd"][oi], (kv, logf)


def trunk(x, c, prm, even_mixer, odd_mixer):
    new_even, new_odd = [], []
    for layer in range(DEPTH):
        mod = jax.nn.silu(c) @ prm["w_ada"][layer] + prm["b_ada"][layer]
        sh1, sc1, g1, sh2, sc2, g2 = [m[:, None, :] for m in jnp.split(mod, 6, axis=-1)]
        h = rms_norm(x, prm["norm_mix_g"][layer]) * (1.0 + sc1) + sh1
        if layer % 2 == 0:
            out, st = even_mixer(h, layer // 2)
            new_even.append(st)
        else:
            out, st = odd_mixer(h, layer // 2)
            new_odd.append(st)
        x = x + g1 * out
        h = rms_norm(x, prm["norm_ffn_g"][layer]) * (1.0 + sc2) + sh2
        x = x + g2 * swiglu(h, prm["w_gate_up"][layer], prm["w_down"][layer])
    return x, new_even, new_odd


def setup_inputs(seed: int = 0) -> dict:
    key = jax.random.key(seed)
    keys = iter([jax.random.fold_in(key, i) for i in range(64)])

    def nrm(shape, scale=1.0):
        return jax.random.normal(next(keys), shape, jnp.float32) * scale

    def gain(shape):
        return 1.0 + nrm(shape, 0.02)

    n_pages = PAST_LEN // PAGE_SIZE
    n_pool = (DEC_BATCH * n_pages * 5) // 4
    page_table = jax.random.permutation(next(keys), n_pool)[:DEC_BATCH * n_pages].reshape(
        DEC_BATCH, n_pages).astype(jnp.int32)
    f_cols = jnp.concatenate([jnp.ones((IN_ODD - FOX_HEADS,), jnp.float32),
                              jnp.full((FOX_HEADS,), 0.1, jnp.float32)])
    return {
        "x_prompt": nrm((BATCH, SEQ, D_MODEL)),
        "x_sample": nrm((DEC_BATCH, DEC_SEQ, D_MODEL)),
        "c_prompt": nrm((BATCH, D_MODEL)),
        "c_sample": nrm((DEC_BATCH, D_MODEL)),
        "cache_moba_kv": nrm((N_EVEN, n_pool, PAGE_SIZE, 2, MOBA_KV_HEADS, HEAD_DIM)),
        "cache_mla": nrm((N_EVEN, n_pool, PAGE_SIZE, MLA_KV_LORA + MLA_ROPE)),
        "cache_fox_kv": nrm((N_ODD, n_pool, PAGE_SIZE, 2, FOX_KV_HEADS, HEAD_DIM)),
        "cache_fox_logf": jax.nn.log_sigmoid(nrm((N_ODD, n_pool, PAGE_SIZE, FOX_HEADS)) + 3.0),
        "page_table": page_table,
        "rpe_table": nrm((RPE_BUCKETS, MOBA_HEADS), 0.5),
        "norm_mix_g": gain((DEPTH, D_MODEL)),
        "norm_ffn_g": gain((DEPTH, D_MODEL)),
        "w_ada": nrm((DEPTH, D_MODEL, 6 * D_MODEL), D_MODEL ** -0.5),
        "b_ada": nrm((DEPTH, 6 * D_MODEL), 0.02),
        "w_in_even": nrm((N_EVEN, D_MODEL, IN_EVEN), D_MODEL ** -0.5),
        "moba_q_g": gain((N_EVEN, HEAD_DIM)),
        "moba_k_g": gain((N_EVEN, HEAD_DIM)),
        "mla_q_a_g": gain((N_EVEN, MLA_Q_LORA)),
        "w_uq": nrm((N_EVEN, MLA_Q_LORA, MLA_HEADS * MLA_QK), MLA_Q_LORA ** -0.5),
        "mla_kv_a_g": gain((N_EVEN, MLA_KV_LORA)),
        "w_ukv": nrm((N_EVEN, MLA_KV_LORA, MLA_HEADS * (MLA_NOPE + MLA_V)), MLA_KV_LORA ** -0.5),
        "mla_q_g": gain((N_EVEN, MLA_NOPE + MLA_ROPE // 2)),
        "mla_k_g": gain((N_EVEN, MLA_NOPE + MLA_ROPE // 2)),
        "w_out_even": nrm((N_EVEN, MIX_EVEN, D_MODEL), MIX_EVEN ** -0.5),
        "w_in_odd": nrm((N_ODD, D_MODEL, IN_ODD), D_MODEL ** -0.5) * f_cols,
        "fox_f_b": jax.random.uniform(next(keys), (N_ODD, FOX_HEADS), jnp.float32, 1.0, 6.0),
        "fox_q_g": gain((N_ODD, HEAD_DIM)),
        "fox_k_g": gain((N_ODD, HEAD_DIM)),
        "w_out_odd": nrm((N_ODD, MIX_ODD, D_MODEL), MIX_ODD ** -0.5),
        "w_gate_up": nrm((DEPTH, D_MODEL, 2 * D_FF), D_MODEL ** -0.5),
        "w_down": nrm((DEPTH, D_FF, D_MODEL), D_FF ** -0.5),
    }


def reference(x_prompt, x_sample, c_prompt, c_sample, cache_moba_kv, cache_mla, cache_fox_kv,
              cache_fox_logf, page_table, rpe_table, norm_mix_g, norm_ffn_g, w_ada, b_ada,
              w_in_even, moba_q_g, moba_k_g, mla_q_a_g, w_uq, mla_kv_a_g, w_ukv, mla_q_g, mla_k_g,
              w_out_even, w_in_odd, fox_f_b, fox_q_g, fox_k_g, w_out_odd, w_gate_up, w_down):
    prm = {
        "rpe_table": rpe_table, "norm_mix_g": norm_mix_g, "norm_ffn_g": norm_ffn_g,
        "w_ada": w_ada, "b_ada": b_ada, "w_in_even": w_in_even, "moba_q_g": moba_q_g,
        "moba_k_g": moba_k_g, "mla_q_a_g": mla_q_a_g, "w_uq": w_uq, "mla_kv_a_g": mla_kv_a_g,
        "w_ukv": w_ukv, "mla_q_g": mla_q_g, "mla_k_g": mla_k_g, "w_out_even": w_out_even,
        "w_in_odd": w_in_odd, "fox_f_b": fox_f_b, "fox_q_g": fox_q_g, "fox_k_g": fox_k_g,
        "w_out_odd": w_out_odd, "w_gate_up": w_gate_up, "w_down": w_down,
    }
    y_prompt, ev_p, od_p = trunk(
        x_prompt, c_prompt, prm,
        lambda h, i: even_mixer_prompt(h, i, prm),
        lambda h, i: odd_mixer_prompt(h, i, prm))
    y_sample, ev_s, od_s = trunk(
        x_sample, c_sample, prm,
        lambda h, i: even_mixer_sample(h, i, prm, cache_moba_kv, cache_mla, page_table),
        lambda h, i: odd_mixer_sample(h, i, prm, cache_fox_kv, cache_fox_logf, page_table))
    new_moba_kv_prompt = jnp.stack([e[0] for e in ev_p])
    new_moba_kv_sample = jnp.stack([e[0] for e in ev_s])
    new_mla_prompt = jnp.stack([e[1] for e in ev_p])
    new_mla_sample = jnp.stack([e[1] for e in ev_s])
    new_fox_kv_prompt = jnp.stack([o[0] for o in od_p])
    new_fox_kv_sample = jnp.stack([o[0] for o in od_s])
    new_fox_logf_prompt = jnp.stack([o[1] for o in od_p])
    new_fox_logf_sample = jnp.stack([o[1] for o in od_s])
    return (y_prompt, y_sample, new_moba_kv_prompt, new_moba_kv_sample, new_mla_prompt,
            new_mla_sample, new_fox_kv_prompt, new_fox_kv_sample, new_fox_logf_prompt,
            new_fox_logf_sample)
```

```python
import functools
import math

import numpy as np
import jax
import jax.numpy as jnp
from jax import lax
from jax.experimental import pallas as pl
from jax.experimental.pallas import tpu as pltpu

f32 = jnp.float32
bf16 = jnp.bfloat16

HEAD_DIM = 128
MOBA_KV_HEADS = 2
MOBA_BLOCK = 256
MOBA_TOPK = 3
MLA_NOPE = 128
MLA_ROPE = 64
MLA_V = 128
MLA_QK = MLA_NOPE + MLA_ROPE
ROPE_THETA = 10000.0
FOX_KV_HEADS = 2
RPE_BUCKETS = 32
RPE_MAX_DIST = 128
PAGE_SIZE = 128
EPS = 1e-6
NEG_INF = -1e30
ATTN_SCALE = HEAD_DIM ** -0.5
MLA_SCALE = MLA_QK ** -0.5

LANES = 128
VMEM_LIMIT = 56 * 1024 * 1024
PAGES_PER_STEP = 16

_NT = (((1,), (1,)), ((), ()))


def _dot(a, b):
    return jnp.dot(a, b, preferred_element_type=f32)


def _dot_nt(a, b):
    return lax.dot_general(a, b, _NT, preferred_element_type=f32)


def _split3(x):
    hi = x.astype(bf16)
    r1 = x - hi.astype(f32)
    mid = r1.astype(bf16)
    lo = (r1 - mid.astype(f32)).astype(bf16)
    return hi, mid, lo


def _rms(x, g):
    return x * lax.rsqrt(jnp.mean(x * x, axis=-1, keepdims=True) + EPS) * g


def _params(*sem):
    return pltpu.CompilerParams(dimension_semantics=sem, vmem_limit_bytes=VMEM_LIMIT)


def _mm_kernel(*refs, nk, swiglu, has_bias, has_res, silu_in):
    it = iter(refs)
    a_ref = next(it)
    w_ref = next(it)
    w2_ref = next(it) if swiglu else None
    b_ref = next(it) if has_bias else None
    x_ref = next(it) if has_res else None
    g_ref = next(it) if has_res else None
    o_ref = next(it)
    accs = list(it)

    a = a_ref[...]
    if silu_in:
        a = a * jax.nn.sigmoid(a)
    a = a.astype(bf16)

    def finish(acc, acc2):
        r = acc
        if swiglu:
            r = (acc * jax.nn.sigmoid(acc)) * acc2
        if has_bias:
            r = r + b_ref[...]
        if has_res:
            r = x_ref[...] + g_ref[0] * r
        o_ref[...] = r.astype(o_ref.dtype)

    if nk == 1:
        finish(_dot(a, w_ref[...]), _dot(a, w2_ref[...]) if swiglu else None)
        return

    k = pl.program_id(2)

    @pl.when(k == 0)
    def _():
        for acc in accs:
            acc[...] = jnp.zeros_like(acc)

    accs[0][...] += _dot(a, w_ref[...])
    if swiglu:
        accs[1][...] += _dot(a, w2_ref[...])

    @pl.when(k == nk - 1)
    def _():
        finish(accs[0][...], accs[1][...] if swiglu else None)


def _matmul(a, w, *, out_dtype, tm, tn, tk=None, swiglu=False, bias=None, res=None,
            gate=None, silu_in=False):
    m, kd = a.shape
    n = w.shape[1] // 2 if swiglu else w.shape[1]
    tm = min(tm, m)
    tn = min(tn, n)
    tk = kd if tk is None else tk
    assert m % tm == 0 and n % tn == 0 and kd % tk == 0
    nk = kd // tk
    nj = n // tn
    in_specs = [pl.BlockSpec((tm, tk), lambda i, j, k: (i, k)),
                pl.BlockSpec((tk, tn), lambda i, j, k: (k, j))]
    args = [a, w]
    if swiglu:
        in_specs.append(pl.BlockSpec((tk, tn), lambda i, j, k: (k, j + nj)))
        args.append(w)
    if bias is not None:
        in_specs.append(pl.BlockSpec((1, tn), lambda i, j, k: (0, j)))
        args.append(bias)
    if res is not None:
        tiles_per_group = (m // tm) // gate.shape[0]
        in_specs.append(pl.BlockSpec((tm, tn), lambda i, j, k: (i, j)))
        in_specs.append(pl.BlockSpec((1, gate.shape[1], tn),
                                     lambda i, j, k: (i // tiles_per_group, 0, j)))
        args += [res, gate]
    scratch = []
    if nk > 1:
        scratch = [pltpu.VMEM((tm, tn), f32)] * (2 if swiglu else 1)
    return pl.pallas_call(
        functools.partial(_mm_kernel, nk=nk, swiglu=swiglu, has_bias=bias is not None,
                          has_res=res is not None, silu_in=silu_in),
        grid=(m // tm, nj, nk),
        in_specs=in_specs,
        out_specs=pl.BlockSpec((tm, tn), lambda i, j, k: (i, j)),
        out_shape=jax.ShapeDtypeStruct((m, n), out_dtype),
        scratch_shapes=scratch,
        compiler_params=_params("parallel", "parallel", "arbitrary"),
    )(*args)


def _norm_mod_kernel(x_ref, g_ref, sc_ref, sh_ref, o_ref):
    y = _rms(x_ref[...], g_ref[...])
    o_ref[...] = (y * (1.0 + sc_ref[0]) + sh_ref[0]).astype(o_ref.dtype)


def _norm_mod(x, g, sc, sh, tm):
    m, d = x.shape
    tm = min(tm, m)
    tiles_per_group = (m // tm) // sc.shape[0]
    mod_spec = pl.BlockSpec((1, sc.shape[1], d), lambda i: (i // tiles_per_group, 0, 0))
    return pl.pallas_call(
        _norm_mod_kernel,
        grid=(m // tm,),
        in_specs=[pl.BlockSpec((tm, d), lambda i: (i, 0)),
                  pl.BlockSpec((1, d), lambda i: (0, 0)), mod_spec, mod_spec],
        out_specs=pl.BlockSpec((tm, d), lambda i: (i, 0)),
        out_shape=jax.ShapeDtypeStruct((m, d), bf16),
        compiler_params=_params("parallel"),
    )(x, g, sc, sh)


def _rope128(x, cos_ref, sin_ref):
    lane = lax.broadcasted_iota(jnp.int32, x.shape, 1)
    swapped = jnp.where(lane < MLA_ROPE // 2, pltpu.roll(x, LANES - MLA_ROPE // 2, 1),
                        pltpu.roll(x, MLA_ROPE // 2, 1))
    return x * cos_ref[...] + swapped * sin_ref[...]


def _even_post_kernel(z_ref, cos_ref, sin_ref, qg_ref, kg_ref, cqg_ref, ckvg_ref,
                      qa_ref, kva_ref, kb_ref, vb_ref, cqn_ref, rows_ref, ckvn_ref, kpe_ref,
                      *maybe_kmean_ref, n_heads, q_lora, kv_lora):
    tm = z_ref.shape[0]
    d = HEAD_DIM
    for h in range(n_heads):
        qa_ref[:, h * d:(h + 1) * d] = _rms(z_ref[:, h * d:(h + 1) * d], qg_ref[...])
    off = n_heads * d
    for h in range(MOBA_KV_HEADS):
        k = _rms(z_ref[:, off + h * d:off + (h + 1) * d], kg_ref[...])
        kva_ref[:, h * d:(h + 1) * d] = k
        kb_ref[:, h * d:(h + 1) * d] = k.astype(bf16)
        for blk in range(tm // MOBA_BLOCK if maybe_kmean_ref else 0):
            maybe_kmean_ref[0][blk, :, h * d:(h + 1) * d] = jnp.mean(
                k[blk * MOBA_BLOCK:(blk + 1) * MOBA_BLOCK], axis=0, keepdims=True)
    off += MOBA_KV_HEADS * d
    v = z_ref[:, off:off + MOBA_KV_HEADS * d]
    kva_ref[:, MOBA_KV_HEADS * d:] = v
    vb_ref[...] = v.astype(bf16)
    off += MOBA_KV_HEADS * d
    cqn_ref[...] = _rms(z_ref[:, off:off + q_lora], cqg_ref[...]).astype(bf16)
    off += q_lora
    ckvn = _rms(z_ref[:, off:off + kv_lora], ckvg_ref[...])
    rows_ref[:, :kv_lora] = ckvn
    ckvn_ref[...] = ckvn.astype(bf16)
    off += kv_lora
    kr = _rope128(z_ref[:, off:off + LANES], cos_ref, sin_ref)
    kpe_ref[...] = kr
    rows_ref[:, kv_lora:] = kr[:, :MLA_ROPE]


def _even_post(z, cos_t, sin_t, prm_g, *, tm, n_heads, q_lora, kv_lora, with_kmean):
    m = z.shape[0]
    tm = min(tm, m)
    t_tiles = cos_t.shape[0] // tm
    d = HEAD_DIM
    row = lambda w: pl.BlockSpec((tm, w), lambda i: (i, 0))
    vec = lambda w: pl.BlockSpec((1, w), lambda i: (0, 0))
    tab = pl.BlockSpec((tm, LANES), lambda i: (i % t_tiles, 0))
    nblk = tm // MOBA_BLOCK
    kvw = MOBA_KV_HEADS * d
    outs = [
        jax.ShapeDtypeStruct((m, n_heads * d), f32),
        jax.ShapeDtypeStruct((m, 2 * kvw), f32),
        jax.ShapeDtypeStruct((m, kvw), bf16),
        jax.ShapeDtypeStruct((m, kvw), bf16),
        jax.ShapeDtypeStruct((m, q_lora), bf16),
        jax.ShapeDtypeStruct((m, kv_lora + MLA_ROPE), f32),
        jax.ShapeDtypeStruct((m, kv_lora), bf16),
        jax.ShapeDtypeStruct((m, LANES), f32),
    ]
    out_specs = [row(n_heads * d), row(2 * kvw), row(kvw), row(kvw), row(q_lora),
                 row(kv_lora + MLA_ROPE), row(kv_lora), row(LANES)]
    if with_kmean:
        assert tm % MOBA_BLOCK == 0
        outs.append(jax.ShapeDtypeStruct((m // MOBA_BLOCK, 1, kvw), f32))
        out_specs.append(pl.BlockSpec((nblk, 1, kvw), lambda i: (i, 0, 0)))
    return pl.pallas_call(
        functools.partial(_even_post_kernel, n_heads=n_heads, q_lora=q_lora, kv_lora=kv_lora),
        grid=(m // tm,),
        in_specs=[row(z.shape[1]), tab, tab, vec(d), vec(d), vec(q_lora), vec(kv_lora)],
        out_specs=out_specs,
        out_shape=outs,
        compiler_params=_params("parallel"),
    )(z, cos_t, sin_t, *prm_g)


def _mla_q_kernel(c_ref, w_ref, cos_ref, sin_ref, g_ref, o_ref):
    acc = _dot(c_ref[...], w_ref[...])
    nope = acc[:, :MLA_NOPE]
    rope = _rope128(acc[:, MLA_NOPE:], cos_ref, sin_ref)
    ss = jnp.sum(nope * nope, axis=-1, keepdims=True) + jnp.sum(rope * rope, axis=-1, keepdims=True)
    inv = lax.rsqrt(ss * (1.0 / MLA_QK) + EPS)
    o_ref[:, :MLA_NOPE] = (nope * inv * g_ref[:, :MLA_NOPE]).astype(o_ref.dtype)
    o_ref[:, MLA_NOPE:] = (rope * inv * g_ref[:, MLA_NOPE:]).astype(o_ref.dtype)


def _mla_q(cqn, w_uq_pad, cos_t, sin_t, gain, *, tm, n_heads):
    m, ql = cqn.shape
    tm = min(tm, m)
    t_tiles = cos_t.shape[0] // tm
    tab = pl.BlockSpec((tm, LANES), lambda i, h: (i % t_tiles, 0))
    return pl.pallas_call(
        _mla_q_kernel,
        grid=(m // tm, n_heads),
        in_specs=[pl.BlockSpec((tm, ql), lambda i, h: (i, 0)),
                  pl.BlockSpec((ql, 2 * LANES), lambda i, h: (0, h)), tab, tab,
                  pl.BlockSpec((1, 2 * LANES), lambda i, h: (0, 0))],
        out_specs=pl.BlockSpec((tm, 2 * LANES), lambda i, h: (i, h)),
        out_shape=jax.ShapeDtypeStruct((m, n_heads * 2 * LANES), bf16),
        compiler_params=_params("parallel", "arbitrary"),
    )(cqn, w_uq_pad, cos_t, sin_t, gain)


def _mla_kv_kernel(c_ref, kpe_ref, w_ref, k_ref, v_ref):
    acc = _dot(c_ref[...], w_ref[...])
    kn = acc[:, :MLA_NOPE]
    kp = kpe_ref[...]
    ss = jnp.sum(kn * kn, axis=-1, keepdims=True) + jnp.sum(kp * kp, axis=-1, keepdims=True)
    inv = lax.rsqrt(ss * (1.0 / MLA_QK) + EPS)
    k_ref[:, :MLA_NOPE] = (kn * inv).astype(bf16)
    k_ref[:, MLA_NOPE:] = (kp * inv).astype(bf16)
    v_ref[...] = acc[:, MLA_NOPE:].astype(bf16)


def _mla_kv(ckvn, kpe, w_ukv, *, tm, n_heads):
    m, kl = ckvn.shape
    tm = min(tm, m)
    return pl.pallas_call(
        _mla_kv_kernel,
        grid=(m // tm, n_heads),
        in_specs=[pl.BlockSpec((tm, kl), lambda i, h: (i, 0)),
                  pl.BlockSpec((tm, LANES), lambda i, h: (i, 0)),
                  pl.BlockSpec((kl, 2 * LANES), lambda i, h: (0, h))],
        out_specs=[pl.BlockSpec((tm, 2 * LANES), lambda i, h: (i, h)),
                   pl.BlockSpec((tm, LANES), lambda i, h: (i, h))],
        out_shape=[jax.ShapeDtypeStruct((m, n_heads * 2 * LANES), bf16),
                   jax.ShapeDtypeStruct((m, n_heads * LANES), bf16)],
        compiler_params=_params("parallel", "arbitrary"),
    )(ckvn, kpe, w_ukv)


def _odd_post_kernel(z_ref, qg_ref, kg_ref, fb_ref, q_ref, kv_ref, kb_ref, vb_ref,
                     logf_ref, cum_ref, carry_ref, *, n_heads, group, tiles_per_seq):
    tm = z_ref.shape[0]
    d = HEAD_DIM
    for h in range(n_heads):
        q_ref[:, h * d:(h + 1) * d] = (_rms(z_ref[:, h * d:(h + 1) * d], qg_ref[...])
                                       * ATTN_SCALE).astype(q_ref.dtype)
    off = n_heads * d
    for h in range(FOX_KV_HEADS):
        k = _rms(z_ref[:, off + h * d:off + (h + 1) * d], kg_ref[...])
        kv_ref[:, h * d:(h + 1) * d] = k
        kb_ref[:, h * d:(h + 1) * d] = k.astype(bf16)
    off += FOX_KV_HEADS * d
    v = z_ref[:, off:off + FOX_KV_HEADS * d]
    kv_ref[:, FOX_KV_HEADS * d:] = v
    vb_ref[...] = v.astype(bf16)
    off += FOX_KV_HEADS * d
    x = z_ref[:, off:off + LANES] + fb_ref[...]
    logf = jnp.minimum(x, 0.0) - jnp.log1p(jnp.exp(-jnp.abs(x)))
    logf_ref[...] = logf[:, :n_heads]

    r = lax.broadcasted_iota(jnp.int32, (tm, tm), 0)
    c = lax.broadcasted_iota(jnp.int32, (tm, tm), 1)
    tri = (c <= r) if group >= tm else ((c <= r) & (c // group == r // group))
    tri = jnp.where(tri, 1.0, 0.0).astype(bf16)
    hi, mid, lo = _split3(logf)
    cum = _dot(tri, hi) + _dot(tri, mid) + _dot(tri, lo)
    if group >= tm:
        first = pl.program_id(0) % tiles_per_seq == 0

        @pl.when(first)
        def _():
            carry_ref[...] = jnp.zeros_like(carry_ref)

        cum = cum + carry_ref[...]
        carry_ref[...] = cum[tm - 1:tm, :]
    cum_ref[...] = cum[:, :n_heads]


def _odd_post(z, qg, kg, fb_pad, *, tm, n_heads, group):
    m = z.shape[0]
    tm = min(tm, m)
    assert group >= tm and group % tm == 0 or tm % group == 0
    d = HEAD_DIM
    kvw = FOX_KV_HEADS * d
    row = lambda w: pl.BlockSpec((tm, w), lambda i: (i, 0))
    vec = lambda w: pl.BlockSpec((1, w), lambda i: (0, 0))
    outs = [jax.ShapeDtypeStruct((m, n_heads * d), bf16),
            jax.ShapeDtypeStruct((m, 2 * kvw), f32),
            jax.ShapeDtypeStruct((m, kvw), bf16),
            jax.ShapeDtypeStruct((m, kvw), bf16),
            jax.ShapeDtypeStruct((m, n_heads), f32),
            jax.ShapeDtypeStruct((m, n_heads), f32)]
    return pl.pallas_call(
        functools.partial(_odd_post_kernel, n_heads=n_heads, group=group,
                          tiles_per_seq=max(group // tm, 1)),
        grid=(m // tm,),
        in_specs=[row(z.shape[1]), vec(d), vec(d), vec(LANES)],
        out_specs=[row(n_heads * d), row(2 * kvw), row(kvw), row(kvw), row(n_heads), row(n_heads)],
        out_shape=outs,
        scratch_shapes=[pltpu.VMEM((1, LANES), f32)],
        compiler_params=_params("arbitrary"),
    )(z, qg, kg, fb_pad)


def _softmax_init(m_ref, l_ref, acc_ref):
    m_ref[...] = jnp.full_like(m_ref, NEG_INF)
    l_ref[...] = jnp.zeros_like(l_ref)
    acc_ref[...] = jnp.zeros_like(acc_ref)


def _softmax_step(s, v, m_ref, l_ref, acc_ref, idx=()):
    idx = idx if idx else Ellipsis
    m_prev = m_ref[idx]
    m_new = jnp.maximum(m_prev, jnp.max(s, axis=-1, keepdims=True))
    alpha = jnp.exp(m_prev - m_new)
    p = jnp.exp(s - m_new)
    l_ref[idx] = alpha * l_ref[idx] + jnp.sum(p, axis=-1, keepdims=True)
    acc_ref[idx] = alpha * acc_ref[idx] + _dot(p.astype(bf16), v)
    m_ref[idx] = m_new


def _causal(tq, tk, q0, k0):
    r = lax.broadcasted_iota(jnp.int32, (tq, tk), 0) + q0
    c = lax.broadcasted_iota(jnp.int32, (tq, tk), 1) + k0
    return r >= c


def _mla_prompt_kernel(q_ref, k_ref, v_ref, o_ref, m_ref, l_ref, acc_ref, *, tq):
    i = pl.program_id(2)
    q = q_ref[0]
    _softmax_init(m_ref, l_ref, acc_ref)

    def body(j, carry):
        ks = pl.ds(pl.multiple_of(j * tq, tq), tq)
        _softmax_step(_dot_nt(q, k_ref[0, ks, :]), v_ref[0, ks, :], m_ref, l_ref, acc_ref)
        return carry

    lax.fori_loop(0, i, body, 0)
    ks = pl.ds(pl.multiple_of(i * tq, tq), tq)
    s = jnp.where(_causal(tq, tq, 0, 0), _dot_nt(q, k_ref[0, ks, :]), NEG_INF)
    _softmax_step(s, v_ref[0, ks, :], m_ref, l_ref, acc_ref)
    o_ref[0] = (acc_ref[...] / l_ref[...]).astype(o_ref.dtype)


def _mla_prompt(q, k, v, *, n_heads, tq):
    b, t, _ = q.shape
    tq = min(tq, t)
    return pl.pallas_call(
        functools.partial(_mla_prompt_kernel, tq=tq),
        grid=(b, n_heads, t // tq),
        in_specs=[pl.BlockSpec((1, tq, 2 * LANES), lambda b_, h, i: (b_, i, h)),
                  pl.BlockSpec((1, t, 2 * LANES), lambda b_, h, i: (b_, 0, h)),
                  pl.BlockSpec((1, t, LANES), lambda b_, h, i: (b_, 0, h))],
        out_specs=pl.BlockSpec((1, tq, LANES), lambda b_, h, i: (b_, i, h)),
        out_shape=jax.ShapeDtypeStruct((b, t, n_heads * LANES), bf16),
        scratch_shapes=[pltpu.VMEM((tq, 1), f32), pltpu.VMEM((tq, 1), f32),
                        pltpu.VMEM((tq, LANES), f32)],
        compiler_params=_params("parallel", "parallel", "arbitrary"),
    )(q, k, v)


def _fox_prompt_kernel(q_ref, k_ref, v_ref, fq_ref, fk_ref, o_ref, m_ref, l_ref, acc_ref,
                       *, tq, group):
    i = pl.program_id(2)
    d = HEAD_DIM
    _softmax_init(m_ref, l_ref, acc_ref)

    def block(j, diag):
        ks = pl.ds(pl.multiple_of(j * tq, tq), tq)
        kb = k_ref[0, ks, :]
        vb = v_ref[0, ks, :]
        for g in range(group):
            s = _dot_nt(q_ref[0, :, g * d:(g + 1) * d], kb)
            s = s + (fq_ref[0, 0, :, g:g + 1] - fk_ref[0, 0, j, g:g + 1, :])
            if diag:
                s = jnp.where(_causal(tq, tq, 0, 0), s, NEG_INF)
            _softmax_step(s, vb, m_ref, l_ref, acc_ref, (g,))

    def body(j, carry):
        block(j, False)
        return carry

    lax.fori_loop(0, i, body, 0)
    block(i, True)
    for g in range(group):
        o_ref[0, :, g * d:(g + 1) * d] = (acc_ref[g] / l_ref[g]).astype(o_ref.dtype)


def _fox_prompt(q, kb, vb, fq, fk, *, tq):
    b, t, hd = q.shape
    group = hd // HEAD_DIM // FOX_KV_HEADS
    assert fk.shape[-1] == tq
    gd = group * HEAD_DIM
    return pl.pallas_call(
        functools.partial(_fox_prompt_kernel, tq=tq, group=group),
        grid=(b, FOX_KV_HEADS, t // tq),
        in_specs=[pl.BlockSpec((1, tq, gd), lambda b_, h, i: (b_, i, h)),
                  pl.BlockSpec((1, t, HEAD_DIM), lambda b_, h, i: (b_, 0, h)),
                  pl.BlockSpec((1, t, HEAD_DIM), lambda b_, h, i: (b_, 0, h)),
                  pl.BlockSpec((1, 1, tq, group), lambda b_, h, i: (b_, h, i, 0)),
                  pl.BlockSpec((1, 1, t // tq, group, tq), lambda b_, h, i: (b_, h, 0, 0, 0))],
        out_specs=pl.BlockSpec((1, tq, gd), lambda b_, h, i: (b_, i, h)),
        out_shape=jax.ShapeDtypeStruct((b, t, hd), bf16),
        scratch_shapes=[pltpu.VMEM((group, tq, 1), f32), pltpu.VMEM((group, tq, 1), f32),
                        pltpu.VMEM((group, tq, HEAD_DIM), f32)],
        compiler_params=_params("parallel", "parallel", "arbitrary"),
    )(q, kb, vb, fq, fk)


def _topk_mask(score, n_valid):
    nb = score.shape[1]
    blk = lax.broadcasted_iota(jnp.int32, score.shape, 1)
    score = jnp.where(blk < n_valid, score, NEG_INF)
    rank = jnp.zeros(score.shape, f32)
    for c in range(nb):
        col = score[:, c:c + 1]
        ahead = jnp.where(col > score, 1.0, jnp.where((col == score) & (blk > c), 1.0, 0.0))
        rank = rank + ahead
    return jnp.where((rank < MOBA_TOPK) & (blk < n_valid), 1.0, 0.0)


def _block_scores(q, km):
    qh, qm, ql = _split3(q)
    kh, kmid, kl = _split3(km)
    return (_dot_nt(qh, kh) + (_dot_nt(qh, kmid) + _dot_nt(qm, kh))
            + (_dot_nt(qh, kl) + _dot_nt(qm, kmid) + _dot_nt(ql, kh)))


def _moba_prompt_kernel(q_ref, k_ref, v_ref, km_ref, bown_ref, bprev_ref, bfar_ref, o_ref,
                        sel_ref, m_ref, l_ref, acc_ref, *, group):
    i = pl.program_id(2)
    d = HEAD_DIM
    tq = MOBA_BLOCK
    nb = km_ref.shape[1]
    _softmax_init(m_ref, l_ref, acc_ref)
    km = km_ref[0]
    for g in range(group):
        sel_ref[g] = _topk_mask(_block_scores(q_ref[0, :, g * d:(g + 1) * d], km), i)

    def block(j, kind):
        ks = pl.ds(pl.multiple_of(j * tq, tq), tq)
        kb = k_ref[0, ks, :]
        vb = v_ref[0, ks, :]
        for g in range(group):
            qs = (q_ref[0, :, g * d:(g + 1) * d] * ATTN_SCALE).astype(bf16)
            s = _dot_nt(qs, kb)
            if kind == "own":
                s = jnp.where(_causal(tq, tq, 0, 0), s + bown_ref[g], NEG_INF)
            else:
                s = s + (bprev_ref[g] if kind == "prev" else bfar_ref[g])
                lane = lax.broadcasted_iota(jnp.int32, (tq, nb), 1)
                picked = jnp.sum(jnp.where(lane == j, sel_ref[g], 0.0), axis=-1, keepdims=True)
                s = jnp.where(picked > 0.5, s, NEG_INF)
            _softmax_step(s, vb, m_ref, l_ref, acc_ref, (g,))

    block(i, "own")

    @pl.when(i >= 1)
    def _():
        block(i - 1, "prev")

    def body(j, carry):
        block(j, "far")
        return carry

    lax.fori_loop(0, i - 1, body, 0)
    for g in range(group):
        o_ref[0, :, g * d:(g + 1) * d] = (acc_ref[g] / l_ref[g]).astype(o_ref.dtype)


def _moba_prompt(qa, kb, vb, kmean, b_own, b_prev, b_far):
    b, t, hd = qa.shape
    group = hd // HEAD_DIM // MOBA_KV_HEADS
    nb = t // MOBA_BLOCK
    gd = group * HEAD_DIM
    blk = MOBA_BLOCK
    bias_spec = pl.BlockSpec((group, blk, blk), lambda b_, h, i: (h, 0, 0))
    return pl.pallas_call(
        functools.partial(_moba_prompt_kernel, group=group),
        grid=(b, MOBA_KV_HEADS, nb),
        in_specs=[pl.BlockSpec((1, blk, gd), lambda b_, h, i: (b_, i, h)),
                  pl.BlockSpec((1, t, HEAD_DIM), lambda b_, h, i: (b_, 0, h)),
                  pl.BlockSpec((1, t, HEAD_DIM), lambda b_, h, i: (b_, 0, h)),
                  pl.BlockSpec((1, nb, HEAD_DIM), lambda b_, h, i: (b_, 0, h)),
                  bias_spec, bias_spec,
                  pl.BlockSpec((group, 1, blk), lambda b_, h, i: (h, 0, 0))],
        out_specs=pl.BlockSpec((1, blk, gd), lambda b_, h, i: (b_, i, h)),
        out_shape=jax.ShapeDtypeStruct((b, t, hd), bf16),
        scratch_shapes=[pltpu.VMEM((group, blk, nb), f32),
                        pltpu.VMEM((group, blk, 1), f32), pltpu.VMEM((group, blk, 1), f32),
                        pltpu.VMEM((group, blk, HEAD_DIM), f32)],
        compiler_params=_params("parallel", "parallel", "arbitrary"),
    )(qa, kb, vb, kmean, b_own, b_prev, b_far)


def _pages_per_step(n_pages):
    return PAGES_PER_STEP if n_pages % PAGES_PER_STEP == 0 else n_pages


def _page_specs(width, col, n_pages, pps, reverse):
    def spec(t):
        def index(n, c, pt):
            p = c * pps + t
            return (0, pt[n, n_pages - 1 - p if reverse else p], 0, col)
        return pl.BlockSpec((None, None, PAGE_SIZE, width), index)
    return [spec(t) for t in range(pps)]


def _per_seq(shape):
    nd = len(shape)
    return pl.BlockSpec((1,) + tuple(shape[1:]), lambda n, c, pt: (n,) + (0,) * (nd - 1))


def _shared(shape):
    nd = len(shape)
    return pl.BlockSpec(tuple(shape), lambda n, c, pt: (0,) * nd)


def _new_key_mask(rows, group, s_new):
    r = lax.broadcasted_iota(jnp.int32, (rows, LANES), 0) // group
    j = lax.broadcasted_iota(jnp.int32, (rows, LANES), 1)
    return (j <= r) & (j < s_new)


def _moba_select_kernel(pt_ref, q_ref, *rest, pps):
    pages = rest[:pps]
    sel_ref, km_ref = rest[pps:]
    c = pl.program_id(1)
    nb = km_ref.shape[0]
    d = HEAD_DIM
    for t in range(0, pps, 2):
        tot = (jnp.sum(pages[t][...], axis=0, keepdims=True)
               + jnp.sum(pages[t + 1][...], axis=0, keepdims=True))
        km_ref[pl.ds(c * (pps // 2) + t // 2, 1), :] = tot * (1.0 / MOBA_BLOCK)

    @pl.when(c == pl.num_programs(1) - 1)
    def _():
        for kh in range(MOBA_KV_HEADS):
            score = _block_scores(q_ref[0, kh], km_ref[:, kh * d:(kh + 1) * d])
            sel_ref[0, kh] = _topk_mask(score, nb)


def _moba_select(page_table, q, cache):
    n, kh, r, d = q.shape
    n_pages = page_table.shape[1]
    pps = _pages_per_step(n_pages)
    nb = n_pages // 2
    return pl.pallas_call(
        functools.partial(_moba_select_kernel, pps=pps),
        grid_spec=pltpu.PrefetchScalarGridSpec(
            num_scalar_prefetch=1,
            grid=(n, n_pages // pps),
            in_specs=[_per_seq(q.shape)] + _page_specs(kh * d, 0, n_pages, pps, False),
            out_specs=_per_seq((n, kh, r, nb)),
            scratch_shapes=[pltpu.VMEM((nb, kh * d), f32)]),
        out_shape=jax.ShapeDtypeStruct((n, kh, r, nb), f32),
        compiler_params=_params("parallel", "arbitrary"),
    )(page_table, q, *([cache] * pps))


def _moba_sample_kernel(pt_ref, q_ref, sel_ref, blast_ref, bfar_ref, bown_ref, new_ref, *rest,
                        pps, group, s_new):
    pages = rest[:pps]
    o_ref, m_ref, l_ref, acc_ref = rest[pps:]
    c = pl.program_id(1)
    d = HEAD_DIM
    kvw = MOBA_KV_HEADS * d
    rows, nb = sel_ref.shape[2], sel_ref.shape[3]

    @pl.when(c == 0)
    def _():
        _softmax_init(m_ref, l_ref, acc_ref)

    qs = [(q_ref[0, kh] * ATTN_SCALE).astype(bf16) for kh in range(MOBA_KV_HEADS)]
    lane = lax.broadcasted_iota(jnp.int32, (rows, nb), 1)
    for t in range(0, pps, 2):
        blk = c * (pps // 2) + t // 2
        for kh in range(MOBA_KV_HEADS):
            cols = slice(kh * d, (kh + 1) * d)
            vcols = slice(kvw + kh * d, kvw + (kh + 1) * d)
            kb = jnp.concatenate([pages[t][:, cols], pages[t + 1][:, cols]], axis=0).astype(bf16)
            vb = jnp.concatenate([pages[t][:, vcols], pages[t + 1][:, vcols]], axis=0).astype(bf16)
            bias = jnp.where(blk == nb - 1, blast_ref[kh], bfar_ref[kh])
            picked = jnp.sum(jnp.where(lane == blk, sel_ref[0, kh], 0.0), axis=-1, keepdims=True)
            s = jnp.where(picked > 0.5, _dot_nt(qs[kh], kb) + bias, NEG_INF)
            _softmax_step(s, vb, m_ref, l_ref, acc_ref, (kh,))

    @pl.when(c == pl.num_programs(1) - 1)
    def _():
        mask = _new_key_mask(rows, group, s_new)
        for kh in range(MOBA_KV_HEADS):
            kb = new_ref[0, :, kh * d:(kh + 1) * d].astype(bf16)
            vb = new_ref[0, :, kvw + kh * d:kvw + (kh + 1) * d].astype(bf16)
            s = jnp.where(mask, _dot_nt(qs[kh], kb) + bown_ref[kh], NEG_INF)
            _softmax_step(s, vb, m_ref, l_ref, acc_ref, (kh,))
            o_ref[0, kh] = (acc_ref[kh] / l_ref[kh]).astype(o_ref.dtype)


def _moba_sample(page_table, q, sel, b_last, b_far, b_own, kv_new, cache, *, group, s_new):
    n, kh, r, d = q.shape
    n_pages = page_table.shape[1]
    pps = _pages_per_step(n_pages)
    return pl.pallas_call(
        functools.partial(_moba_sample_kernel, pps=pps, group=group, s_new=s_new),
        grid_spec=pltpu.PrefetchScalarGridSpec(
            num_scalar_prefetch=1,
            grid=(n, n_pages // pps),
            in_specs=[_per_seq(q.shape), _per_seq(sel.shape), _shared(b_last.shape),
                      _shared(b_far.shape), _shared(b_own.shape), _per_seq(kv_new.shape)]
            + _page_specs(2 * kh * d, 0, n_pages, pps, False),
            out_specs=_per_seq(q.shape),
            scratch_shapes=[pltpu.VMEM((kh, r, 1), f32), pltpu.VMEM((kh, r, 1), f32),
                            pltpu.VMEM((kh, r, d), f32)]),
        out_shape=jax.ShapeDtypeStruct(q.shape, bf16),
        compiler_params=_params("parallel", "arbitrary"),
    )(page_table, q, sel, b_last, b_far, b_own, kv_new, *([cache] * pps))


def _mla_absorb_kernel(q_ref, w_ref, o_ref):
    o_ref[...] = _dot(q_ref[...], w_ref[...]).astype(o_ref.dtype)


def _mla_absorb(qb, w_uk_t, n_heads):
    m = qb.shape[0]
    kl = w_uk_t.shape[1]
    return pl.pallas_call(
        _mla_absorb_kernel,
        grid=(n_heads,),
        in_specs=[pl.BlockSpec((m, MLA_NOPE), lambda h: (0, 2 * h)),
                  pl.BlockSpec((MLA_NOPE, kl), lambda h: (h, 0))],
        out_specs=pl.BlockSpec((m, kl), lambda h: (0, h)),
        out_shape=jax.ShapeDtypeStruct((m, n_heads * kl), bf16),
        compiler_params=_params("parallel"),
    )(qb, w_uk_t)


def _mla_sample_kernel(pt_ref, ql_ref, qr_ref, wk_ref, wv_ref, new_ref, *rest,
                       pps, n_heads, s_new, kv_lora):
    pages = rest[:pps]
    o_ref, m_ref, l_ref, acc_ref = rest[pps:]
    c = pl.program_id(1)
    rows = s_new * n_heads
    ones = jnp.ones((n_heads, MLA_ROPE), bf16)

    @pl.when(c == 0)
    def _():
        _softmax_init(m_ref, l_ref, acc_ref)

    def attend(x, mask):
        r = x[:, :kv_lora].astype(bf16)
        kp = x[:, kv_lora:]
        kn_t = _dot_nt(wk_ref[...], r)
        ssq = jnp.concatenate(
            [jnp.sum(jnp.square(kn_t[h * MLA_NOPE:(h + 1) * MLA_NOPE]), axis=0, keepdims=True)
             for h in range(n_heads)], axis=0)
        kp2 = kp * kp
        kp2_hi = kp2.astype(bf16)
        kp2_lo = (kp2 - kp2_hi.astype(f32)).astype(bf16)
        ssq = ssq + (_dot_nt(ones, kp2_hi) + _dot_nt(ones, kp2_lo))
        inv = lax.rsqrt(ssq * (1.0 / MLA_QK) + EPS)
        s = _dot_nt(ql_ref[0], r) + _dot_nt(qr_ref[0], kp.astype(bf16))
        s = s * jnp.concatenate([inv] * s_new, axis=0)
        if mask is not None:
            s = jnp.where(mask, s, NEG_INF)
        _softmax_step(s, r, m_ref, l_ref, acc_ref)

    for t in range(0, pps, 2):
        attend(jnp.concatenate([pages[t][...], pages[t + 1][...]], axis=0), None)

    @pl.when(c == pl.num_programs(1) - 1)
    def _():
        attend(new_ref[0], _new_key_mask(rows, n_heads, s_new))
        lat = (acc_ref[...] / l_ref[...]).astype(bf16)
        full = _dot(lat, wv_ref[...])
        head = lax.broadcasted_iota(jnp.int32, (rows, MLA_V), 0) % n_heads
        out = jnp.zeros((rows, MLA_V), f32)
        for h in range(n_heads):
            out = out + jnp.where(head == h, full[:, h * MLA_V:(h + 1) * MLA_V], 0.0)
        o_ref[0] = out.astype(o_ref.dtype)


def _mla_sample(page_table, q_lat, q_rope, w_uk_t, w_uv, rows_new, cache, *, n_heads, s_new):
    n, r, kl = q_lat.shape
    n_pages = page_table.shape[1]
    pps = _pages_per_step(n_pages)
    return pl.pallas_call(
        functools.partial(_mla_sample_kernel, pps=pps, n_heads=n_heads, s_new=s_new, kv_lora=kl),
        grid_spec=pltpu.PrefetchScalarGridSpec(
            num_scalar_prefetch=1,
            grid=(n, n_pages // pps),
            in_specs=[_per_seq(q_lat.shape), _per_seq(q_rope.shape), _shared(w_uk_t.shape),
                      _shared(w_uv.shape), _per_seq(rows_new.shape)]
            + _page_specs(kl + MLA_ROPE, 0, n_pages, pps, False),
            out_specs=_per_seq((n, r, MLA_V)),
            scratch_shapes=[pltpu.VMEM((r, 1), f32), pltpu.VMEM((r, 1), f32),
                            pltpu.VMEM((r, kl), f32)]),
        out_shape=jax.ShapeDtypeStruct((n, r, MLA_V), bf16),
        compiler_params=_params("parallel", "arbitrary"),
    )(page_table, q_lat, q_rope, w_uk_t, w_uv, rows_new, *([cache] * pps))


def _fox_sample_kernel(pt_ref, q_ref, fn_ref, fnl_ref, new_ref, *rest, pps, group, s_new):
    kv_pages = rest[:pps]
    lf_pages = rest[pps:2 * pps]
    o_ref, m_ref, l_ref, acc_ref, carry_ref = rest[2 * pps:]
    c = pl.program_id(1)
    d = HEAD_DIM
    kvw = FOX_KV_HEADS * d
    n_heads = FOX_KV_HEADS * group
    rows = s_new * group

    @pl.when(c == 0)
    def _():
        _softmax_init(m_ref, l_ref, acc_ref)
        carry_ref[...] = jnp.zeros_like(carry_ref)

    eye = jnp.where(lax.broadcasted_iota(jnp.int32, (n_heads, n_heads), 0)
                    == lax.broadcasted_iota(jnp.int32, (n_heads, n_heads), 1), 1.0, 0.0).astype(bf16)
    later = jnp.where(lax.broadcasted_iota(jnp.int32, (PAGE_SIZE, PAGE_SIZE), 0)
                      > lax.broadcasted_iota(jnp.int32, (PAGE_SIZE, PAGE_SIZE), 1), 1.0, 0.0).astype(bf16)
    carry = carry_ref[...]
    for t in range(pps):
        hi, mid, lo = _split3(lf_pages[t][...])
        lf_t = _dot_nt(eye, hi) + _dot_nt(eye, mid) + _dot_nt(eye, lo)
        hi, mid, lo = _split3(lf_t)
        suffix = _dot(hi, later) + _dot(mid, later) + _dot(lo, later)
        after = suffix + carry
        carry = carry + (suffix[:, 0:1] + lf_t[:, 0:1])
        for kh in range(FOX_KV_HEADS):
            kb = kv_pages[t][:, kh * d:(kh + 1) * d].astype(bf16)
            vb = kv_pages[t][:, kvw + kh * d:kvw + (kh + 1) * d].astype(bf16)
            bias = jnp.concatenate([after[kh * group:(kh + 1) * group]] * s_new, axis=0)
            s = _dot_nt(q_ref[0, kh], kb) + (bias + fn_ref[0, kh])
            _softmax_step(s, vb, m_ref, l_ref, acc_ref, (kh,))
    carry_ref[...] = carry

    @pl.when(c == pl.num_programs(1) - 1)
    def _():
        mask = _new_key_mask(rows, group, s_new)
        for kh in range(FOX_KV_HEADS):
            kb = new_ref[0, :, kh * d:(kh + 1) * d].astype(bf16)
            vb = new_ref[0, :, kvw + kh * d:kvw + (kh + 1) * d].astype(bf16)
            s = _dot_nt(q_ref[0, kh], kb) + (fn_ref[0, kh] - fnl_ref[0, kh])
            _softmax_step(jnp.where(mask, s, NEG_INF), vb, m_ref, l_ref, acc_ref, (kh,))
            o_ref[0, kh] = (acc_ref[kh] / l_ref[kh]).astype(o_ref.dtype)


def _fox_sample(page_table, q, fn, fnl, kv_new, cache_kv, cache_logf, *, group, s_new):
    n, kh, r, d = q.shape
    n_pages = page_table.shape[1]
    pps = _pages_per_step(n_pages)
    n_heads = kh * group
    return pl.pallas_call(
        functools.partial(_fox_sample_kernel, pps=pps, group=group, s_new=s_new),
        grid_spec=pltpu.PrefetchScalarGridSpec(
            num_scalar_prefetch=1,
            grid=(n, n_pages // pps),
            in_specs=[_per_seq(q.shape), _per_seq(fn.shape), _per_seq(fnl.shape),
                      _per_seq(kv_new.shape)]
            + _page_specs(2 * kh * d, 0, n_pages, pps, True)
            + _page_specs(n_heads, 0, n_pages, pps, True),
            out_specs=_per_seq(q.shape),
            scratch_shapes=[pltpu.VMEM((kh, r, 1), f32), pltpu.VMEM((kh, r, 1), f32),
                            pltpu.VMEM((kh, r, d), f32), pltpu.VMEM((n_heads, LANES), f32)]),
        out_shape=jax.ShapeDtypeStruct(q.shape, bf16),
        compiler_params=_params("parallel", "arbitrary"),
    )(page_table, q, fn, fnl, kv_new, *([cache_kv] * pps), *([cache_logf] * pps))


def _t5_bucket(dist):
    n = jnp.maximum(dist, 0)
    exact = RPE_BUCKETS // 2
    nf = jnp.maximum(n, 1).astype(f32)
    log_b = exact + (jnp.log(nf / exact) / math.log(RPE_MAX_DIST / exact)
                     * (RPE_BUCKETS - exact)).astype(jnp.int32)
    return jnp.where(n < exact, n, jnp.minimum(log_b, RPE_BUCKETS - 1))


def _rope_tables(pos):
    inv = ROPE_THETA ** (-jnp.arange(0, MLA_ROPE, 2, dtype=f32) / MLA_ROPE)
    ang = pos.astype(f32)[:, None] * inv[None, :]
    cos, sin = jnp.cos(ang), jnp.sin(ang)
    z = jnp.zeros((pos.shape[0], LANES - MLA_ROPE), f32)
    return jnp.concatenate([cos, cos, z], axis=1), jnp.concatenate([-sin, sin, z], axis=1)


def _pad_cols(w, width):
    return jnp.pad(w, ((0, 0), (0, width - w.shape[1])))


def _pad_rows(a, rows):
    return jnp.pad(a, ((0, 0), (0, rows - a.shape[1]), (0, 0)))


def _to_kv_rows(a, n, s, kvh, group):
    return a.reshape(n, s, kvh, group, HEAD_DIM).transpose(0, 2, 1, 3, 4).reshape(
        n, kvh, s * group, HEAD_DIM)


def _from_kv_rows(a, n, s, kvh, group):
    return a.reshape(n, kvh, s, group, HEAD_DIM).transpose(0, 2, 1, 3, 4).reshape(
        n * s, kvh * group * HEAD_DIM)


def kernel(x_prompt, x_sample, c_prompt, c_sample, cache_moba_kv, cache_mla, cache_fox_kv,
           cache_fox_logf, page_table, rpe_table, norm_mix_g, norm_ffn_g, w_ada, b_ada,
           w_in_even, moba_q_g, moba_k_g, mla_q_a_g, w_uq, mla_kv_a_g, w_ukv, mla_q_g, mla_k_g,
           w_out_even, w_in_odd, fox_f_b, fox_q_g, fox_k_g, w_out_odd, w_gate_up, w_down):
    nb_, t, dm = x_prompt.shape
    n, s_new, _ = x_sample.shape
    n_pages = page_table.shape[1]
    past = n_pages * PAGE_SIZE
    assert t % MOBA_BLOCK == 0 and n_pages % 2 == 0 and MOBA_BLOCK >= RPE_MAX_DIST
    moba_heads = dm // (2 * HEAD_DIM)
    mla_heads = dm // (2 * HEAD_DIM)
    fox_heads = dm // HEAD_DIM
    q_lora, kv_lora = dm // 4, dm // 8
    moba_group = moba_heads // MOBA_KV_HEADS
    fox_group = fox_heads // FOX_KV_HEADS
    n_moba_blocks = past // MOBA_BLOCK
    tm = 512

    w_in_even_p = _pad_cols(w_in_even[0], w_in_even.shape[2] + LANES - MLA_ROPE).astype(bf16)
    w_uq_p = jnp.pad(w_uq[0].reshape(q_lora, mla_heads, MLA_QK),
                     ((0, 0), (0, 0), (0, 2 * LANES - MLA_QK))).reshape(q_lora, -1).astype(bf16)
    w_ukv_b = w_ukv[0].astype(bf16)
    w_ukv_h = w_ukv[0].reshape(kv_lora, mla_heads, MLA_NOPE + MLA_V)
    w_uk_t = w_ukv_h[:, :, :MLA_NOPE].reshape(kv_lora, -1).T.astype(bf16)
    w_uv = w_ukv_h[:, :, MLA_NOPE:].reshape(kv_lora, -1).astype(bf16)
    w_in_odd_p = _pad_cols(w_in_odd[0], w_in_odd.shape[2] + LANES - fox_heads).astype(bf16)
    fox_b_p = _pad_cols(fox_f_b[0][None], LANES)
    qk_gain = (jnp.concatenate([mla_q_g[0], mla_q_g[0][MLA_NOPE:]])
               * jnp.concatenate([mla_k_g[0], mla_k_g[0][MLA_NOPE:]]) * MLA_SCALE)
    qk_gain = _pad_cols(qk_gain[None], 2 * LANES)
    even_gains = (moba_q_g[0][None], moba_k_g[0][None], mla_q_a_g[0][None], mla_kv_a_g[0][None])

    i = jnp.arange(MOBA_BLOCK)
    dist = i[:, None] - i[None, :]
    to_heads = lambda b: b.transpose(2, 0, 1)
    b_own = to_heads(rpe_table[_t5_bucket(dist)])
    b_prev = to_heads(rpe_table[_t5_bucket(dist + MOBA_BLOCK)])
    far = rpe_table[_t5_bucket(jnp.array(MOBA_BLOCK + 1))]
    b_far = jnp.broadcast_to(far[:, None, None], (moba_heads, 1, MOBA_BLOCK))
    s_idx = jnp.arange(s_new)
    d_last = (past + s_idx)[:, None] - ((n_moba_blocks - 1) * MOBA_BLOCK + i)[None, :]
    sb_last = rpe_table[_t5_bucket(d_last)].transpose(2, 0, 1)
    d_own = s_idx[:, None] - jnp.arange(LANES)[None, :]
    sb_own = rpe_table[_t5_bucket(d_own)].transpose(2, 0, 1)
    rows_sg = lambda b: b.reshape(MOBA_KV_HEADS, moba_group, s_new, -1).transpose(0, 2, 1, 3).reshape(
        MOBA_KV_HEADS, s_new * moba_group, -1)
    sb_last, sb_own = rows_sg(sb_last), rows_sg(sb_own)
    sb_far = jnp.broadcast_to(rows_sg(jnp.broadcast_to(far[:, None, None], (moba_heads, s_new, 1))),
                              (MOBA_KV_HEADS, s_new * moba_group, MOBA_BLOCK))

    cos_p, sin_p = _rope_tables(jnp.arange(t))
    cos_s, sin_s = _rope_tables(past + jnp.arange(s_new))
    cos_s, sin_s = jnp.tile(cos_s, (n, 1)), jnp.tile(sin_s, (n, 1))

    caches = dict(
        moba=cache_moba_kv.reshape(cache_moba_kv.shape[0], -1, PAGE_SIZE, 4 * HEAD_DIM),
        mla=cache_mla,
        fox=cache_fox_kv.reshape(cache_fox_kv.shape[0], -1, PAGE_SIZE, 4 * HEAD_DIM),
        logf=cache_fox_logf)

    c_all = jnp.concatenate([c_prompt, c_sample], axis=0)

    def modulation(layer):
        mod = _matmul(c_all, w_ada[layer].astype(bf16), out_dtype=f32, tm=c_all.shape[0], tn=1024,
                      bias=b_ada[layer][None], silu_in=True)
        parts = jnp.split(mod, 6, axis=-1)
        prompt = [p[:nb_].reshape(nb_, 1, dm) for p in parts]
        sample = [jnp.repeat(p[nb_:], s_new, axis=0).reshape(1, n * s_new, dm) for p in parts]
        return prompt, sample

    def ffn(x, layer, sc, sh, gate):
        h = _norm_mod(x, norm_ffn_g[layer][None], sc, sh, tm)
        a = _matmul(h, w_gate_up[layer].astype(bf16), out_dtype=bf16, tm=1024, tn=512, swiglu=True)
        d_ff = a.shape[1]
        tk = d_ff // 2 if (d_ff // 2) % LANES == 0 else d_ff
        return _matmul(a, w_down[layer].astype(bf16), out_dtype=f32, tm=1024, tn=512, tk=tk,
                       res=x, gate=gate)

    def even_front(x, sc, sh, cos_t, sin_t, with_kmean):
        h = _norm_mod(x, norm_mix_g[0][None], sc, sh, tm)
        z = _matmul(h, w_in_even_p, out_dtype=f32, tm=tm, tn=w_in_even_p.shape[1])
        qa, kva, kb, vb, cqn, rows, ckvn, kpe, *kmean = _even_post(
            z, cos_t, sin_t, even_gains, tm=tm, n_heads=moba_heads, q_lora=q_lora, kv_lora=kv_lora,
            with_kmean=with_kmean)
        qb = _mla_q(cqn, w_uq_p, cos_t, sin_t, qk_gain, tm=tm, n_heads=mla_heads)
        return qa, kva, kb, vb, rows, ckvn, kpe, kmean, qb

    def odd_front(x, sc, sh, group_rows):
        h = _norm_mod(x, norm_mix_g[1][None], sc, sh, tm)
        z = _matmul(h, w_in_odd_p, out_dtype=f32, tm=tm, tn=w_in_odd_p.shape[1])
        return _odd_post(z, fox_q_g[0][None], fox_k_g[0][None], fox_b_p, tm=tm, n_heads=fox_heads,
                         group=group_rows)

    (sh1, sc1, g1, sh2, sc2, g2), mods_s0 = modulation(0)
    x = x_prompt.reshape(nb_ * t, dm)
    qa, kva, kb, vb, rows, ckvn, kpe, kmean, qb = even_front(x, sc1, sh1, cos_p, sin_p, True)
    seq = lambda a: a.reshape(nb_, t, -1)
    o_a = _moba_prompt(seq(qa), seq(kb), seq(vb), kmean[0].reshape(nb_, t // MOBA_BLOCK, -1),
                       b_own, b_prev, b_far)
    k_b, v_b = _mla_kv(ckvn, kpe, w_ukv_b, tm=tm, n_heads=mla_heads)
    o_b = _mla_prompt(seq(qb), seq(k_b), seq(v_b), n_heads=mla_heads, tq=512)
    o = jnp.concatenate([o_a, o_b], axis=-1).reshape(nb_ * t, -1)
    x = _matmul(o, w_out_even[0].astype(bf16), out_dtype=f32, tm=1024, tn=512, res=x, gate=g1)
    x = ffn(x, 0, sc2, sh2, g2)
    new_moba_p = kva.reshape(1, nb_, t, 2, MOBA_KV_HEADS, HEAD_DIM)
    new_mla_p = rows.reshape(1, nb_, t, -1)

    (sh1, sc1, g1, sh2, sc2, g2), mods_s1 = modulation(1)
    q, kv, kb, vb, logf, cum = odd_front(x, sc1, sh1, t)
    tq = min(256, t)
    cum_h = cum.reshape(nb_, t, FOX_KV_HEADS, fox_group)
    fq = cum_h.transpose(0, 2, 1, 3)
    fk = cum_h.reshape(nb_, t // tq, tq, FOX_KV_HEADS, fox_group).transpose(0, 3, 1, 4, 2)
    o = _fox_prompt(seq(q), seq(kb), seq(vb), fq, fk, tq=tq).reshape(nb_ * t, -1)
    x = _matmul(o, w_out_odd[0].astype(bf16), out_dtype=f32, tm=1024, tn=512, res=x, gate=g1)
    y_prompt = ffn(x, 1, sc2, sh2, g2).reshape(nb_, t, dm)
    new_fox_p = kv.reshape(1, nb_, t, 2, FOX_KV_HEADS, HEAD_DIM)
    new_logf_p = logf.reshape(1, nb_, t, fox_heads)

    sh1, sc1, g1, sh2, sc2, g2 = mods_s0
    x = x_sample.reshape(n * s_new, dm)
    qa, kva, kb, vb, rows, ckvn, kpe, _, qb = even_front(x, sc1, sh1, cos_s, sin_s, False)
    qa_r = _to_kv_rows(qa, n, s_new, MOBA_KV_HEADS, moba_group)
    sel = _moba_select(page_table, qa_r, caches["moba"])
    kva_pad = _pad_rows(kva.reshape(n, s_new, -1), PAGE_SIZE)
    o_a = _moba_sample(page_table, qa_r, sel, sb_last, sb_far, sb_own, kva_pad, caches["moba"],
                       group=moba_group, s_new=s_new)
    o_a = _from_kv_rows(o_a, n, s_new, MOBA_KV_HEADS, moba_group)
    q_lat = _mla_absorb(qb, w_uk_t, mla_heads).reshape(n, s_new * mla_heads, kv_lora)
    q_rope = qb.reshape(n, s_new * mla_heads, 2 * LANES)[:, :, MLA_NOPE:MLA_QK]
    rows_pad = _pad_rows(rows.reshape(n, s_new, -1), PAGE_SIZE)
    o_b = _mla_sample(page_table, q_lat, q_rope, w_uk_t, w_uv, rows_pad, caches["mla"],
                      n_heads=mla_heads, s_new=s_new).reshape(n * s_new, -1)
    o = jnp.concatenate([o_a, o_b], axis=-1)
    x = _matmul(o, w_out_even[0].astype(bf16), out_dtype=f32, tm=1024, tn=512, res=x, gate=g1)
    x = ffn(x, 0, sc2, sh2, g2)
    new_moba_s = kva.reshape(1, n, s_new, 2, MOBA_KV_HEADS, HEAD_DIM)
    new_mla_s = rows.reshape(1, n, s_new, -1)

    sh1, sc1, g1, sh2, sc2, g2 = mods_s1
    q, kv, kb, vb, logf, cum = odd_front(x, sc1, sh1, s_new)
    q_r = _to_kv_rows(q, n, s_new, FOX_KV_HEADS, fox_group)
    cum_s = cum.reshape(n, s_new, FOX_KV_HEADS, fox_group)
    fn = cum_s.transpose(0, 2, 1, 3).reshape(n, FOX_KV_HEADS, s_new * fox_group, 1)
    fnl = jnp.broadcast_to(cum_s.transpose(0, 2, 3, 1)[:, :, None],
                           (n, FOX_KV_HEADS, s_new, fox_group, s_new)).reshape(
        n, FOX_KV_HEADS, s_new * fox_group, s_new)
    fnl = jnp.pad(fnl, ((0, 0), (0, 0), (0, 0), (0, LANES - s_new)))
    kv_pad = _pad_rows(kv.reshape(n, s_new, -1), PAGE_SIZE)
    o = _fox_sample(page_table, q_r, fn, fnl, kv_pad, caches["fox"], caches["logf"],
                    group=fox_group, s_new=s_new)
    o = _from_kv_rows(o, n, s_new, FOX_KV_HEADS, fox_group)
    x = _matmul(o, w_out_odd[0].astype(bf16), out_dtype=f32, tm=1024, tn=512, res=x, gate=g1)
    y_sample = ffn(x, 1, sc2, sh2, g2).reshape(n, s_new, dm)
    new_fox_s = kv.reshape(1, n, s_new, 2, FOX_KV_HEADS, HEAD_DIM)
    new_logf_s = logf.reshape(1, n, s_new, fox_heads)

    return (y_prompt, y_sample, new_moba_p, new_moba_s, new_mla_p, new_mla_s,
            new_fox_p, new_fox_s, new_logf_p, new_logf_s)
```

```python
import functools
import math

import numpy as np
import jax
import jax.numpy as jnp
from jax import lax
from jax.experimental import pallas as pl
from jax.experimental.pallas import tpu as pltpu

f32 = jnp.float32
bf16 = jnp.bfloat16

HEAD_DIM = 128
MOBA_KV_HEADS = 2
MOBA_BLOCK = 256
MOBA_TOPK = 3
MLA_NOPE = 128
MLA_ROPE = 64
MLA_V = 128
MLA_QK = MLA_NOPE + MLA_ROPE
ROPE_THETA = 10000.0
FOX_KV_HEADS = 2
RPE_BUCKETS = 32
RPE_MAX_DIST = 128
PAGE_SIZE = 128
EPS = 1e-6
NEG_INF = -1e30
ATTN_SCALE = HEAD_DIM ** -0.5
MLA_SCALE = MLA_QK ** -0.5

LANES = 128
VMEM_LIMIT = 56 * 1024 * 1024
PAGES_PER_STEP = 16

_NT = (((1,), (1,)), ((), ()))


def _dot(a, b):
    return jnp.dot(a, b, preferred_element_type=f32)


def _dot_nt(a, b):
    return lax.dot_general(a, b, _NT, preferred_element_type=f32)


def _split3(x):
    hi = x.astype(bf16)
    r1 = x - hi.astype(f32)
    mid = r1.astype(bf16)
    lo = (r1 - mid.astype(f32)).astype(bf16)
    return hi, mid, lo


def _rms(x, g):
    return x * lax.rsqrt(jnp.mean(x * x, axis=-1, keepdims=True) + EPS) * g


def _params(*sem):
    return pltpu.CompilerParams(dimension_semantics=sem, vmem_limit_bytes=VMEM_LIMIT)


def _mm_kernel(*refs, nk, swiglu, has_bias, has_res, silu_in):
    it = iter(refs)
    a_ref = next(it)
    w_ref = next(it)
    w2_ref = next(it) if swiglu else None
    b_ref = next(it) if has_bias else None
    x_ref = next(it) if has_res else None
    g_ref = next(it) if has_res else None
    o_ref = next(it)
    accs = list(it)

    a = a_ref[...]
    if silu_in:
        a = a * jax.nn.sigmoid(a)
    a = a.astype(bf16)

    def finish(acc, acc2):
        r = acc
        if swiglu:
            r = (acc * jax.nn.sigmoid(acc)) * acc2
        if has_bias:
            r = r + b_ref[...]
        if has_res:
            r = x_ref[...] + g_ref[0] * r
        o_ref[...] = r.astype(o_ref.dtype)

    if nk == 1:
        finish(_dot(a, w_ref[...]), _dot(a, w2_ref[...]) if swiglu else None)
        return

    k = pl.program_id(2)

    @pl.when(k == 0)
    def _():
        for acc in accs:
            acc[...] = jnp.zeros_like(acc)

    accs[0][...] += _dot(a, w_ref[...])
    if swiglu:
        accs[1][...] += _dot(a, w2_ref[...])

    @pl.when(k == nk - 1)
    def _():
        finish(accs[0][...], accs[1][...] if swiglu else None)


def _matmul(a, w, *, out_dtype, tm, tn, tk=None, swiglu=False, bias=None, res=None,
            gate=None, silu_in=False):
    m, kd = a.shape
    n = w.shape[1] // 2 if swiglu else w.shape[1]
    tm = min(tm, m)
    tn = min(tn, n)
    tk = kd if tk is None else tk
    assert m % tm == 0 and n % tn == 0 and kd % tk == 0
    nk = kd // tk
    nj = n // tn
    in_specs = [pl.BlockSpec((tm, tk), lambda i, j, k: (i, k)),
                pl.BlockSpec((tk, tn), lambda i, j, k: (k, j))]
    args = [a, w]
    if swiglu:
        in_specs.append(pl.BlockSpec((tk, tn), lambda i, j, k: (k, j + nj)))
        args.append(w)
    if bias is not None:
        in_specs.append(pl.BlockSpec((1, tn), lambda i, j, k: (0, j)))
        args.append(bias)
    if res is not None:
        tiles_per_group = (m // tm) // gate.shape[0]
        in_specs.append(pl.BlockSpec((tm, tn), lambda i, j, k: (i, j)))
        in_specs.append(pl.BlockSpec((1, gate.shape[1], tn),
                                     lambda i, j, k: (i // tiles_per_group, 0, j)))
        args += [res, gate]
    scratch = []
    if nk > 1:
        scratch = [pltpu.VMEM((tm, tn), f32)] * (2 if swiglu else 1)
    return pl.pallas_call(
        functools.partial(_mm_kernel, nk=nk, swiglu=swiglu, has_bias=bias is not None,
                          has_res=res is not None, silu_in=silu_in),
        grid=(m // tm, nj, nk),
        in_specs=in_specs,
        out_specs=pl.BlockSpec((tm, tn), lambda i, j, k: (i, j)),
        out_shape=jax.ShapeDtypeStruct((m, n), out_dtype),
        scratch_shapes=scratch,
        compiler_params=_params("parallel", "parallel", "arbitrary"),
    )(*args)


def _norm_mod_kernel(x_ref, g_ref, sc_ref, sh_ref, o_ref):
    y = _rms(x_ref[...], g_ref[...])
    o_ref[...] = (y * (1.0 + sc_ref[0]) + sh_ref[0]).astype(o_ref.dtype)


def _norm_mod(x, g, sc, sh, tm):
    m, d = x.shape
    tm = min(tm, m)
    tiles_per_group = (m // tm) // sc.shape[0]
    mod_spec = pl.BlockSpec((1, sc.shape[1], d), lambda i: (i // tiles_per_group, 0, 0))
    return pl.pallas_call(
        _norm_mod_kernel,
        grid=(m // tm,),
        in_specs=[pl.BlockSpec((tm, d), lambda i: (i, 0)),
                  pl.BlockSpec((1, d), lambda i: (0, 0)), mod_spec, mod_spec],
        out_specs=pl.BlockSpec((tm, d), lambda i: (i, 0)),
        out_shape=jax.ShapeDtypeStruct((m, d), bf16),
        compiler_params=_params("parallel"),
    )(x, g, sc, sh)


def _rope128(x, cos_ref, sin_ref):
    lane = lax.broadcasted_iota(jnp.int32, x.shape, 1)
    swapped = jnp.where(lane < MLA_ROPE // 2, pltpu.roll(x, LANES - MLA_ROPE // 2, 1),
                        pltpu.roll(x, MLA_ROPE // 2, 1))
    return x * cos_ref[...] + swapped * sin_ref[...]


def _store_with_ones(vb_ref, v, n_heads):
    d = HEAD_DIM
    for h in range(n_heads):
        vb_ref[:, 2 * h * d:(2 * h + 1) * d] = v[:, h * d:(h + 1) * d].astype(bf16)
        vb_ref[:, (2 * h + 1) * d:(2 * h + 2) * d] = jnp.ones((v.shape[0], d), bf16)


def _even_post_kernel(z_ref, cos_ref, sin_ref, qg_ref, kg_ref, cqg_ref, ckvg_ref,
                      qa_ref, kva_ref, kb_ref, vb_ref, cqn_ref, rows_ref, ckvn_ref, kpe_ref,
                      *maybe_kmean_ref, n_heads, q_lora, kv_lora):
    tm = z_ref.shape[0]
    d = HEAD_DIM
    for h in range(n_heads):
        qa_ref[:, h * d:(h + 1) * d] = _rms(z_ref[:, h * d:(h + 1) * d], qg_ref[...])
    off = n_heads * d
    for h in range(MOBA_KV_HEADS):
        k = _rms(z_ref[:, off + h * d:off + (h + 1) * d], kg_ref[...])
        kva_ref[:, h * d:(h + 1) * d] = k
        kb_ref[:, h * d:(h + 1) * d] = k.astype(bf16)
        for blk in range(tm // MOBA_BLOCK if maybe_kmean_ref else 0):
            maybe_kmean_ref[0][blk, :, h * d:(h + 1) * d] = jnp.mean(
                k[blk * MOBA_BLOCK:(blk + 1) * MOBA_BLOCK], axis=0, keepdims=True)
    off += MOBA_KV_HEADS * d
    v = z_ref[:, off:off + MOBA_KV_HEADS * d]
    kva_ref[:, MOBA_KV_HEADS * d:] = v
    _store_with_ones(vb_ref, v, MOBA_KV_HEADS)
    off += MOBA_KV_HEADS * d
    cqn_ref[...] = _rms(z_ref[:, off:off + q_lora], cqg_ref[...]).astype(bf16)
    off += q_lora
    ckvn = _rms(z_ref[:, off:off + kv_lora], ckvg_ref[...])
    rows_ref[:, :kv_lora] = ckvn
    ckvn_ref[...] = ckvn.astype(bf16)
    off += kv_lora
    kr = _rope128(z_ref[:, off:off + LANES], cos_ref, sin_ref)
    kpe_ref[...] = kr
    rows_ref[:, kv_lora:] = kr[:, :MLA_ROPE]


def _even_post(z, cos_t, sin_t, prm_g, *, tm, n_heads, q_lora, kv_lora, with_kmean):
    m = z.shape[0]
    tm = min(tm, m)
    t_tiles = cos_t.shape[0] // tm
    d = HEAD_DIM
    row = lambda w: pl.BlockSpec((tm, w), lambda i: (i, 0))
    vec = lambda w: pl.BlockSpec((1, w), lambda i: (0, 0))
    tab = pl.BlockSpec((tm, LANES), lambda i: (i % t_tiles, 0))
    nblk = tm // MOBA_BLOCK
    kvw = MOBA_KV_HEADS * d
    outs = [
        jax.ShapeDtypeStruct((m, n_heads * d), f32),
        jax.ShapeDtypeStruct((m, 2 * kvw), f32),
        jax.ShapeDtypeStruct((m, kvw), bf16),
        jax.ShapeDtypeStruct((m, 2 * kvw), bf16),
        jax.ShapeDtypeStruct((m, q_lora), bf16),
        jax.ShapeDtypeStruct((m, kv_lora + MLA_ROPE), f32),
        jax.ShapeDtypeStruct((m, kv_lora), bf16),
        jax.ShapeDtypeStruct((m, LANES), f32),
    ]
    out_specs = [row(n_heads * d), row(2 * kvw), row(kvw), row(2 * kvw), row(q_lora),
                 row(kv_lora + MLA_ROPE), row(kv_lora), row(LANES)]
    if with_kmean:
        assert tm % MOBA_BLOCK == 0
        outs.append(jax.ShapeDtypeStruct((m // MOBA_BLOCK, 1, kvw), f32))
        out_specs.append(pl.BlockSpec((nblk, 1, kvw), lambda i: (i, 0, 0)))
    return pl.pallas_call(
        functools.partial(_even_post_kernel, n_heads=n_heads, q_lora=q_lora, kv_lora=kv_lora),
        grid=(m // tm,),
        in_specs=[row(z.shape[1]), tab, tab, vec(d), vec(d), vec(q_lora), vec(kv_lora)],
        out_specs=out_specs,
        out_shape=outs,
        compiler_params=_params("parallel"),
    )(z, cos_t, sin_t, *prm_g)


def _mla_q_kernel(c_ref, w_ref, cos_ref, sin_ref, g_ref, o_ref):
    acc = _dot(c_ref[...], w_ref[...])
    nope = acc[:, :MLA_NOPE]
    rope = _rope128(acc[:, MLA_NOPE:], cos_ref, sin_ref)
    ss = jnp.sum(nope * nope, axis=-1, keepdims=True) + jnp.sum(rope * rope, axis=-1, keepdims=True)
    inv = lax.rsqrt(ss * (1.0 / MLA_QK) + EPS)
    o_ref[:, :MLA_NOPE] = (nope * inv * g_ref[:, :MLA_NOPE]).astype(o_ref.dtype)
    o_ref[:, MLA_NOPE:] = (rope * inv * g_ref[:, MLA_NOPE:]).astype(o_ref.dtype)


def _mla_q(cqn, w_uq_pad, cos_t, sin_t, gain, *, tm, n_heads):
    m, ql = cqn.shape
    tm = min(tm, m)
    t_tiles = cos_t.shape[0] // tm
    tab = pl.BlockSpec((tm, LANES), lambda i, h: (i % t_tiles, 0))
    return pl.pallas_call(
        _mla_q_kernel,
        grid=(m // tm, n_heads),
        in_specs=[pl.BlockSpec((tm, ql), lambda i, h: (i, 0)),
                  pl.BlockSpec((ql, 2 * LANES), lambda i, h: (0, h)), tab, tab,
                  pl.BlockSpec((1, 2 * LANES), lambda i, h: (0, 0))],
        out_specs=pl.BlockSpec((tm, 2 * LANES), lambda i, h: (i, h)),
        out_shape=jax.ShapeDtypeStruct((m, n_heads * 2 * LANES), bf16),
        compiler_params=_params("parallel", "arbitrary"),
    )(cqn, w_uq_pad, cos_t, sin_t, gain)


def _mla_kv_kernel(c_ref, kpe_ref, w_ref, k_ref, v_ref):
    acc = _dot(c_ref[...], w_ref[...])
    kn = acc[:, :MLA_NOPE]
    kp = kpe_ref[...]
    ss = jnp.sum(kn * kn, axis=-1, keepdims=True) + jnp.sum(kp * kp, axis=-1, keepdims=True)
    inv = lax.rsqrt(ss * (1.0 / MLA_QK) + EPS)
    k_ref[:, :MLA_NOPE] = (kn * inv).astype(bf16)
    k_ref[:, MLA_NOPE:] = (kp * inv).astype(bf16)
    v_ref[:, :MLA_V] = acc[:, MLA_NOPE:].astype(bf16)
    v_ref[:, MLA_V:] = jnp.ones((acc.shape[0], LANES), bf16)


def _mla_kv(ckvn, kpe, w_ukv, *, tm, n_heads):
    m, kl = ckvn.shape
    tm = min(tm, m)
    return pl.pallas_call(
        _mla_kv_kernel,
        grid=(m // tm, n_heads),
        in_specs=[pl.BlockSpec((tm, kl), lambda i, h: (i, 0)),
                  pl.BlockSpec((tm, LANES), lambda i, h: (i, 0)),
                  pl.BlockSpec((kl, 2 * LANES), lambda i, h: (0, h))],
        out_specs=[pl.BlockSpec((tm, 2 * LANES), lambda i, h: (i, h)),
                   pl.BlockSpec((tm, MLA_V + LANES), lambda i, h: (i, h))],
        out_shape=[jax.ShapeDtypeStruct((m, n_heads * 2 * LANES), bf16),
                   jax.ShapeDtypeStruct((m, n_heads * (MLA_V + LANES)), bf16)],
        compiler_params=_params("parallel", "arbitrary"),
    )(ckvn, kpe, w_ukv)


def _odd_post_kernel(z_ref, qg_ref, kg_ref, fb_ref, q_ref, kv_ref, kb_ref, vb_ref,
                     logf_ref, cum_ref, carry_ref, *, n_heads, group, tiles_per_seq):
    tm = z_ref.shape[0]
    d = HEAD_DIM
    for h in range(n_heads):
        q_ref[:, h * d:(h + 1) * d] = (_rms(z_ref[:, h * d:(h + 1) * d], qg_ref[...])
                                       * ATTN_SCALE).astype(q_ref.dtype)
    off = n_heads * d
    for h in range(FOX_KV_HEADS):
        k = _rms(z_ref[:, off + h * d:off + (h + 1) * d], kg_ref[...])
        kv_ref[:, h * d:(h + 1) * d] = k
        kb_ref[:, h * d:(h + 1) * d] = k.astype(bf16)
    off += FOX_KV_HEADS * d
    v = z_ref[:, off:off + FOX_KV_HEADS * d]
    kv_ref[:, FOX_KV_HEADS * d:] = v
    _store_with_ones(vb_ref, v, FOX_KV_HEADS)
    off += FOX_KV_HEADS * d
    x = z_ref[:, off:off + LANES] + fb_ref[...]
    logf = jnp.minimum(x, 0.0) - jnp.log1p(jnp.exp(-jnp.abs(x)))
    logf_ref[...] = logf[:, :n_heads]

    r = lax.broadcasted_iota(jnp.int32, (tm, tm), 0)
    c = lax.broadcasted_iota(jnp.int32, (tm, tm), 1)
    tri = (c <= r) if group >= tm else ((c <= r) & (c // group == r // group))
    tri = jnp.where(tri, 1.0, 0.0).astype(bf16)
    hi, mid, lo = _split3(logf)
    cum = _dot(tri, hi) + _dot(tri, mid) + _dot(tri, lo)
    if group >= tm:
        first = pl.program_id(0) % tiles_per_seq == 0

        @pl.when(first)
        def _():
            carry_ref[...] = jnp.zeros_like(carry_ref)

        cum = cum + carry_ref[...]
        carry_ref[...] = cum[tm - 1:tm, :]
    cum_ref[...] = cum[:, :n_heads]


def _odd_post(z, qg, kg, fb_pad, *, tm, n_heads, group):
    m = z.shape[0]
    tm = min(tm, m)
    assert group >= tm and group % tm == 0 or tm % group == 0
    d = HEAD_DIM
    kvw = FOX_KV_HEADS * d
    row = lambda w: pl.BlockSpec((tm, w), lambda i: (i, 0))
    vec = lambda w: pl.BlockSpec((1, w), lambda i: (0, 0))
    outs = [jax.ShapeDtypeStruct((m, n_heads * d), bf16),
            jax.ShapeDtypeStruct((m, 2 * kvw), f32),
            jax.ShapeDtypeStruct((m, kvw), bf16),
            jax.ShapeDtypeStruct((m, 2 * kvw), bf16),
            jax.ShapeDtypeStruct((m, n_heads), f32),
            jax.ShapeDtypeStruct((m, n_heads), f32)]
    return pl.pallas_call(
        functools.partial(_odd_post_kernel, n_heads=n_heads, group=group,
                          tiles_per_seq=max(group // tm, 1)),
        grid=(m // tm,),
        in_specs=[row(z.shape[1]), vec(d), vec(d), vec(LANES)],
        out_specs=[row(n_heads * d), row(2 * kvw), row(kvw), row(2 * kvw), row(n_heads), row(n_heads)],
        out_shape=outs,
        scratch_shapes=[pltpu.VMEM((1, LANES), f32)],
        compiler_params=_params("arbitrary"),
    )(z, qg, kg, fb_pad)


def _softmax_init(m_ref, l_ref, acc_ref):
    m_ref[...] = jnp.full_like(m_ref, NEG_INF)
    if l_ref is not None:
        l_ref[...] = jnp.zeros_like(l_ref)
    acc_ref[...] = jnp.zeros_like(acc_ref)


def _normalised(acc, dv):
    return acc[:, :dv] / acc[:, dv:dv + LANES][:, :dv]


def _softmax_step(s, v, m_ref, l_ref, acc_ref, idx=(), nt=False):
    idx = idx if idx else Ellipsis
    m_prev = m_ref[idx]
    m_new = jnp.maximum(m_prev, jnp.max(s, axis=-1, keepdims=True))
    alpha = jnp.exp(m_prev - m_new)
    p = jnp.exp(s - m_new)
    if l_ref is not None:
        l_ref[idx] = alpha * l_ref[idx] + jnp.sum(p, axis=-1, keepdims=True)
    pb = p.astype(bf16)
    if isinstance(v, (list, tuple)):
        w = s.shape[1] // len(v)
        pv = _dot(pb[:, :w], v[0])
        for t in range(1, len(v)):
            pv = pv + _dot(pb[:, t * w:(t + 1) * w], v[t])
    else:
        pv = _dot_nt(pb, v) if nt else _dot(pb, v)
    acc_ref[idx] = alpha * acc_ref[idx] + pv
    m_ref[idx] = m_new


def _with_ones(v):
    return jnp.concatenate([v, jnp.ones((v.shape[0], LANES), v.dtype)], axis=1)


def _causal(tq, tk, q0, k0):
    r = lax.broadcasted_iota(jnp.int32, (tq, tk), 0) + q0
    c = lax.broadcasted_iota(jnp.int32, (tq, tk), 1) + k0
    return r >= c


def _mla_prompt_kernel(q_ref, k_ref, v_ref, o_ref, m_ref, acc_ref, *, tq):
    i = pl.program_id(2)
    q = q_ref[0]
    _softmax_init(m_ref, None, acc_ref)

    def body(j, carry):
        ks = pl.ds(pl.multiple_of(j * tq, tq), tq)
        _softmax_step(_dot_nt(q, k_ref[0, ks, :]), v_ref[0, ks, :], m_ref, None, acc_ref)
        return carry

    lax.fori_loop(0, i, body, 0)
    ks = pl.ds(pl.multiple_of(i * tq, tq), tq)
    s = jnp.where(_causal(tq, tq, 0, 0), _dot_nt(q, k_ref[0, ks, :]), NEG_INF)
    _softmax_step(s, v_ref[0, ks, :], m_ref, None, acc_ref)
    o_ref[0] = _normalised(acc_ref[...], MLA_V).astype(o_ref.dtype)


def _mla_prompt(q, k, v, *, n_heads, tq):
    b, t, _ = q.shape
    tq = min(tq, t)
    return pl.pallas_call(
        functools.partial(_mla_prompt_kernel, tq=tq),
        grid=(b, n_heads, t // tq),
        in_specs=[pl.BlockSpec((1, tq, 2 * LANES), lambda b_, h, i: (b_, i, h)),
                  pl.BlockSpec((1, t, 2 * LANES), lambda b_, h, i: (b_, 0, h)),
                  pl.BlockSpec((1, t, MLA_V + LANES), lambda b_, h, i: (b_, 0, h))],
        out_specs=pl.BlockSpec((1, tq, MLA_V), lambda b_, h, i: (b_, i, h)),
        out_shape=jax.ShapeDtypeStruct((b, t, n_heads * MLA_V), bf16),
        scratch_shapes=[pltpu.VMEM((tq, 1), f32), pltpu.VMEM((tq, MLA_V + LANES), f32)],
        compiler_params=_params("parallel", "parallel", "arbitrary"),
    )(q, k, v)


def _fox_prompt_kernel(q_ref, k_ref, v_ref, fq_ref, fk_ref, o_ref, m_ref, acc_ref,
                       *, tq, group):
    i = pl.program_id(2)
    d = HEAD_DIM
    _softmax_init(m_ref, None, acc_ref)

    def block(j, diag):
        ks = pl.ds(pl.multiple_of(j * tq, tq), tq)
        kb = k_ref[0, ks, :]
        vb = v_ref[0, ks, :]
        for g in range(group):
            s = _dot_nt(q_ref[0, :, g * d:(g + 1) * d], kb)
            s = s + (fq_ref[0, 0, :, g:g + 1] - fk_ref[0, 0, j, g:g + 1, :])
            if diag:
                s = jnp.where(_causal(tq, tq, 0, 0), s, NEG_INF)
            _softmax_step(s, vb, m_ref, None, acc_ref, (g,))

    def body(j, carry):
        block(j, False)
        return carry

    lax.fori_loop(0, i, body, 0)
    block(i, True)
    for g in range(group):
        o_ref[0, :, g * d:(g + 1) * d] = _normalised(acc_ref[g], d).astype(o_ref.dtype)


def _fox_prompt(q, kb, vb, fq, fk, *, tq):
    b, t, hd = q.shape
    group = hd // HEAD_DIM // FOX_KV_HEADS
    assert fk.shape[-1] == tq
    gd = group * HEAD_DIM
    return pl.pallas_call(
        functools.partial(_fox_prompt_kernel, tq=tq, group=group),
        grid=(b, FOX_KV_HEADS, t // tq),
        in_specs=[pl.BlockSpec((1, tq, gd), lambda b_, h, i: (b_, i, h)),
                  pl.BlockSpec((1, t, HEAD_DIM), lambda b_, h, i: (b_, 0, h)),
                  pl.BlockSpec((1, t, 2 * HEAD_DIM), lambda b_, h, i: (b_, 0, h)),
                  pl.BlockSpec((1, 1, tq, group), lambda b_, h, i: (b_, h, i, 0)),
                  pl.BlockSpec((1, 1, t // tq, group, tq), lambda b_, h, i: (b_, h, 0, 0, 0))],
        out_specs=pl.BlockSpec((1, tq, gd), lambda b_, h, i: (b_, i, h)),
        out_shape=jax.ShapeDtypeStruct((b, t, hd), bf16),
        scratch_shapes=[pltpu.VMEM((group, tq, 1), f32),
                        pltpu.VMEM((group, tq, 2 * HEAD_DIM), f32)],
        compiler_params=_params("parallel", "parallel", "arbitrary"),
    )(q, kb, vb, fq, fk)


def _topk_mask(score, n_valid):
    nb = score.shape[1]
    blk = lax.broadcasted_iota(jnp.int32, score.shape, 1)
    score = jnp.where(blk < n_valid, score, NEG_INF)
    rank = jnp.zeros(score.shape, f32)
    for c in range(nb):
        col = score[:, c:c + 1]
        ahead = jnp.where(col > score, 1.0, jnp.where((col == score) & (blk > c), 1.0, 0.0))
        rank = rank + ahead
    return jnp.where((rank < MOBA_TOPK) & (blk < n_valid), 1.0, 0.0)


def _block_scores(q, km):
    qh, qm, ql = _split3(q)
    kh, kmid, kl = _split3(km)
    return (_dot_nt(qh, kh) + (_dot_nt(qh, kmid) + _dot_nt(qm, kh))
            + (_dot_nt(qh, kl) + _dot_nt(qm, kmid) + _dot_nt(ql, kh)))


def _moba_prompt_kernel(q_ref, k_ref, v_ref, km_ref, bown_ref, bprev_ref, bfar_ref, o_ref,
                        sel_ref, qs_ref, m_ref, acc_ref, *, group):
    i = pl.program_id(2)
    d = HEAD_DIM
    tq = MOBA_BLOCK
    nb = km_ref.shape[1]
    _softmax_init(m_ref, None, acc_ref)
    km = km_ref[0]
    for g in range(group):
        q = q_ref[0, :, g * d:(g + 1) * d]
        sel_ref[g] = _topk_mask(_block_scores(q, km), i)
        qs_ref[g] = (q * ATTN_SCALE).astype(bf16)

    def block(j, kind):
        ks = pl.ds(pl.multiple_of(j * tq, tq), tq)
        kb = k_ref[0, ks, :]
        vb = v_ref[0, ks, :]
        for g in range(group):
            s = _dot_nt(qs_ref[g], kb)
            if kind == "own":
                s = jnp.where(_causal(tq, tq, 0, 0), s + bown_ref[g], NEG_INF)
            else:
                s = s + (bprev_ref[g] if kind == "prev" else bfar_ref[g])
                lane = lax.broadcasted_iota(jnp.int32, (tq, nb), 1)
                picked = jnp.sum(jnp.where(lane == j, sel_ref[g], 0.0), axis=-1, keepdims=True)
                s = jnp.where(picked > 0.5, s, NEG_INF)
            _softmax_step(s, vb, m_ref, None, acc_ref, (g,))

    block(i, "own")

    @pl.when(i >= 1)
    def _():
        block(i - 1, "prev")

    def body(j, carry):
        block(j, "far")
        return carry

    lax.fori_loop(0, i - 1, body, 0)
    for g in range(group):
        o_ref[0, :, g * d:(g + 1) * d] = _normalised(acc_ref[g], d).astype(o_ref.dtype)


def _moba_prompt(qa, kb, vb, kmean, b_own, b_prev, b_far):
    b, t, hd = qa.shape
    group = hd // HEAD_DIM // MOBA_KV_HEADS
    nb = t // MOBA_BLOCK
    gd = group * HEAD_DIM
    blk = MOBA_BLOCK
    bias_spec = pl.BlockSpec((group, blk, blk), lambda b_, h, i: (h, 0, 0))
    return pl.pallas_call(
        functools.partial(_moba_prompt_kernel, group=group),
        grid=(b, MOBA_KV_HEADS, nb),
        in_specs=[pl.BlockSpec((1, blk, gd), lambda b_, h, i: (b_, i, h)),
                  pl.BlockSpec((1, t, HEAD_DIM), lambda b_, h, i: (b_, 0, h)),
                  pl.BlockSpec((1, t, 2 * HEAD_DIM), lambda b_, h, i: (b_, 0, h)),
                  pl.BlockSpec((1, nb, HEAD_DIM), lambda b_, h, i: (b_, 0, h)),
                  bias_spec, bias_spec,
                  pl.BlockSpec((group, 1, blk), lambda b_, h, i: (h, 0, 0))],
        out_specs=pl.BlockSpec((1, blk, gd), lambda b_, h, i: (b_, i, h)),
        out_shape=jax.ShapeDtypeStruct((b, t, hd), bf16),
        scratch_shapes=[pltpu.VMEM((group, blk, nb), f32),
                        pltpu.VMEM((group, blk, HEAD_DIM), bf16),
                        pltpu.VMEM((group, blk, 1), f32),
                        pltpu.VMEM((group, blk, 2 * HEAD_DIM), f32)],
        compiler_params=_params("parallel", "parallel", "arbitrary"),
    )(qa, kb, vb, kmean, b_own, b_prev, b_far)


def _pages_per_step(n_pages):
    return PAGES_PER_STEP if n_pages % PAGES_PER_STEP == 0 else n_pages


def _page_specs(rows, n_pages, pps, reverse):
    def spec(t):
        def index(n, c, pt):
            p = c * pps + t
            return (0, pt[n, n_pages - 1 - p if reverse else p], 0, 0)
        return pl.BlockSpec((None, None, rows, LANES), index)
    return [spec(t) for t in range(pps)]


def _kv_rows(page_ref, which, n_kv_heads):
    return page_ref[pl.ds(which, PAGE_SIZE, stride=2 * n_kv_heads), :]


def _per_seq(shape):
    nd = len(shape)
    return pl.BlockSpec((1,) + tuple(shape[1:]), lambda n, c, pt: (n,) + (0,) * (nd - 1))


def _shared(shape):
    nd = len(shape)
    return pl.BlockSpec(tuple(shape), lambda n, c, pt: (0,) * nd)


def _new_key_mask(rows, group, s_new):
    r = lax.broadcasted_iota(jnp.int32, (rows, LANES), 0) // group
    j = lax.broadcasted_iota(jnp.int32, (rows, LANES), 1)
    return (j <= r) & (j < s_new)


def _moba_sample_kernel(pt_ref, q_ref, blast_ref, bfar_ref, bown_ref, new_ref, *rest,
                        pps, group, s_new):
    pages = rest[:pps]
    o_ref, k_s, v_s, km_s, sel_s, m_ref, l_ref, acc_ref = rest[pps:]
    c = pl.program_id(1)
    nc = pl.num_programs(1)
    d = HEAD_DIM
    kvh = MOBA_KV_HEADS
    rows, nb = sel_s.shape[1], sel_s.shape[2]
    chunk_keys = pps * PAGE_SIZE
    blocks_per_chunk = chunk_keys // MOBA_BLOCK

    for kh in range(kvh):
        sums = []
        for t in range(pps):
            k = _kv_rows(pages[t], kh, kvh)
            v = _kv_rows(pages[t], kvh + kh, kvh)
            r0 = pl.multiple_of((c * pps + t) * PAGE_SIZE, PAGE_SIZE)
            k_s[kh, pl.ds(r0, PAGE_SIZE), :] = k.astype(bf16)
            v_s[kh, pl.ds(r0, PAGE_SIZE), :] = v.astype(bf16)
            sums.append(jnp.sum(k, axis=0, keepdims=True))
        for t in range(0, pps, 2):
            km_s[kh, pl.ds(c * blocks_per_chunk + t // 2, 1), :] = (
                (sums[t] + sums[t + 1]) * (1.0 / MOBA_BLOCK))

    @pl.when(c == nc - 1)
    def _():
        _softmax_init(m_ref, l_ref, acc_ref)
        lane = lax.broadcasted_iota(jnp.int32, (rows, nb), 1)
        for kh in range(kvh):
            sel_s[kh] = _topk_mask(_block_scores(q_ref[0, kh], km_s[kh]), nb)
        qs = [(q_ref[0, kh] * ATTN_SCALE).astype(bf16) for kh in range(kvh)]

        def chunk(ci, carry):
            ks = pl.ds(pl.multiple_of(ci * chunk_keys, chunk_keys), chunk_keys)
            for kh in range(kvh):
                s = _dot_nt(qs[kh], k_s[kh, ks, :])
                bias = jnp.where(ci == nc - 1, blast_ref[kh], bfar_ref[kh])
                sel = sel_s[kh]
                picked = jnp.concatenate(
                    [jnp.broadcast_to(
                        jnp.sum(jnp.where(lane == ci * blocks_per_chunk + b, sel, 0.0),
                                axis=-1, keepdims=True), (rows, MOBA_BLOCK))
                     for b in range(blocks_per_chunk)], axis=1)
                s = jnp.where(picked > 0.5, s + bias, NEG_INF)
                _softmax_step(s, v_s[kh, ks, :], m_ref, l_ref, acc_ref, (kh,))
            return carry

        lax.fori_loop(0, nc, chunk, 0)
        mask = _new_key_mask(rows, group, s_new)
        for kh in range(kvh):
            kb = new_ref[0, :, kh * d:(kh + 1) * d].astype(bf16)
            vb = new_ref[0, :, (kvh + kh) * d:(kvh + kh + 1) * d].astype(bf16)
            s = jnp.where(mask, _dot_nt(qs[kh], kb) + bown_ref[kh], NEG_INF)
            _softmax_step(s, vb, m_ref, l_ref, acc_ref, (kh,))
            o_ref[0, kh] = (acc_ref[kh] / l_ref[kh]).astype(o_ref.dtype)


def _moba_sample(page_table, q, b_last, b_far, b_own, kv_new, cache, *, group, s_new):
    n, kh, r, d = q.shape
    n_pages = page_table.shape[1]
    pps = _pages_per_step(n_pages)
    past = n_pages * PAGE_SIZE
    nb = past // MOBA_BLOCK
    assert pps % 2 == 0
    return pl.pallas_call(
        functools.partial(_moba_sample_kernel, pps=pps, group=group, s_new=s_new),
        grid_spec=pltpu.PrefetchScalarGridSpec(
            num_scalar_prefetch=1,
            grid=(n, n_pages // pps),
            in_specs=[_per_seq(q.shape), _shared(b_last.shape), _shared(b_far.shape),
                      _shared(b_own.shape), _per_seq(kv_new.shape)]
            + _page_specs(PAGE_SIZE * 2 * kh, n_pages, pps, False),
            out_specs=_per_seq(q.shape),
            scratch_shapes=[pltpu.VMEM((kh, past, d), bf16), pltpu.VMEM((kh, past, d), bf16),
                            pltpu.VMEM((kh, nb, d), f32), pltpu.VMEM((kh, r, nb), f32),
                            pltpu.VMEM((kh, r, 1), f32), pltpu.VMEM((kh, r, 1), f32),
                            pltpu.VMEM((kh, r, d), f32)]),
        out_shape=jax.ShapeDtypeStruct(q.shape, bf16),
        compiler_params=_params("arbitrary", "arbitrary"),
    )(page_table, q, b_last, b_far, b_own, kv_new, *([cache] * pps))


def _mla_absorb_kernel(q_ref, w_ref, o_ref):
    o_ref[...] = _dot(q_ref[...], w_ref[...]).astype(o_ref.dtype)


def _mla_absorb(qb, w_uk_t, n_heads):
    m = qb.shape[0]
    kl = w_uk_t.shape[1]
    return pl.pallas_call(
        _mla_absorb_kernel,
        grid=(n_heads,),
        in_specs=[pl.BlockSpec((m, MLA_NOPE), lambda h: (0, 2 * h)),
                  pl.BlockSpec((MLA_NOPE, kl), lambda h: (h, 0))],
        out_specs=pl.BlockSpec((m, kl), lambda h: (0, h)),
        out_shape=jax.ShapeDtypeStruct((m, n_heads * kl), bf16),
        compiler_params=_params("parallel"),
    )(qb, w_uk_t)


def _mla_sample_kernel(pt_ref, ql_ref, qr_ref, wk_ref, wv_ref, new_ref, *rest,
                       pps, n_heads, s_new, kv_lora):
    pages = rest[:pps]
    o_ref, lhs_s, m_ref, l_ref, acc_ref = rest[pps:]
    c = pl.program_id(1)
    rows = s_new * n_heads
    n_up = n_heads * MLA_NOPE

    @pl.when(c == 0)
    def _():
        _softmax_init(m_ref, l_ref, acc_ref)
        lhs_s[:n_up, :] = wk_ref[...]
        lhs_s[n_up:, :] = ql_ref[0]

    def attend(lat_t, kp_t, mask):
        n_keys = lat_t.shape[1]
        step = min(n_keys, 2 * LANES)
        ssq, s_lat = [], []
        for k0 in range(0, n_keys, step):
            both = _dot(lhs_s[...], lat_t[:, k0:k0 + step])
            ssq.append(jnp.concatenate(
                [jnp.sum(jnp.square(both[h * MLA_NOPE:(h + 1) * MLA_NOPE]), axis=0, keepdims=True)
                 for h in range(n_heads)], axis=0))
            s_lat.append(both[n_up:])
        ssq = jnp.concatenate(ssq, axis=1) + jnp.sum(kp_t * kp_t, axis=0, keepdims=True)
        inv = lax.rsqrt(ssq * (1.0 / MLA_QK) + EPS)
        s = jnp.concatenate(s_lat, axis=1) + _dot(qr_ref[0], kp_t.astype(bf16))
        s = s * jnp.concatenate([inv] * s_new, axis=0)
        if mask is not None:
            s = jnp.where(mask, s, NEG_INF)
        _softmax_step(s, lat_t, m_ref, l_ref, acc_ref, nt=True)

    attend(jnp.concatenate([p[:kv_lora, :].astype(bf16) for p in pages], axis=1),
           jnp.concatenate([p[kv_lora:, :] for p in pages], axis=1), None)

    @pl.when(c == pl.num_programs(1) - 1)
    def _():
        attend(new_ref[0, :kv_lora, :].astype(bf16), new_ref[0, kv_lora:, :],
               _new_key_mask(rows, n_heads, s_new))
        lat = (acc_ref[...] / l_ref[...]).astype(bf16)
        full = _dot(lat, wv_ref[...])
        head = lax.broadcasted_iota(jnp.int32, (rows, MLA_V), 0) % n_heads
        out = jnp.zeros((rows, MLA_V), f32)
        for h in range(n_heads):
            out = out + jnp.where(head == h, full[:, h * MLA_V:(h + 1) * MLA_V], 0.0)
        o_ref[0] = out.astype(o_ref.dtype)


def _mla_sample(page_table, q_lat, q_rope, w_uk_t, w_uv, rows_new, cache, *, n_heads, s_new):
    n, r, kl = q_lat.shape
    n_pages = page_table.shape[1]
    pps = _pages_per_step(n_pages)
    return pl.pallas_call(
        functools.partial(_mla_sample_kernel, pps=pps, n_heads=n_heads, s_new=s_new, kv_lora=kl),
        grid_spec=pltpu.PrefetchScalarGridSpec(
            num_scalar_prefetch=1,
            grid=(n, n_pages // pps),
            in_specs=[_per_seq(q_lat.shape), _per_seq(q_rope.shape), _shared(w_uk_t.shape),
                      _shared(w_uv.shape), _per_seq(rows_new.shape)]
            + _page_specs(kl + MLA_ROPE, n_pages, pps, False),
            out_specs=_per_seq((n, r, MLA_V)),
            scratch_shapes=[pltpu.VMEM((w_uk_t.shape[0] + r, kl), bf16),
                            pltpu.VMEM((r, 1), f32), pltpu.VMEM((r, 1), f32),
                            pltpu.VMEM((r, kl), f32)]),
        out_shape=jax.ShapeDtypeStruct((n, r, MLA_V), bf16),
        compiler_params=_params("parallel", "arbitrary"),
    )(page_table, q_lat, q_rope, w_uk_t, w_uv, rows_new, *([cache] * pps))


def _fox_sample_kernel(pt_ref, q_ref, fn_ref, fnl_ref, new_ref, *rest, pps, group, s_new):
    kv_pages = rest[:pps]
    lf_pages = rest[pps:2 * pps]
    o_ref, m_ref, l_ref, acc_ref, carry_ref = rest[2 * pps:]
    c = pl.program_id(1)
    d = HEAD_DIM
    kvw = FOX_KV_HEADS * d
    n_heads = FOX_KV_HEADS * group
    rows = s_new * group

    @pl.when(c == 0)
    def _():
        _softmax_init(m_ref, l_ref, acc_ref)
        carry_ref[...] = jnp.zeros_like(carry_ref)

    later = jnp.where(lax.broadcasted_iota(jnp.int32, (PAGE_SIZE, PAGE_SIZE), 0)
                      > lax.broadcasted_iota(jnp.int32, (PAGE_SIZE, PAGE_SIZE), 1), 1.0, 0.0).astype(bf16)
    parts = [part for t in range(pps) for part in _split3(lf_pages[t][...])]
    sums = _dot(jnp.concatenate(parts, axis=0), later)
    carry = carry_ref[...]
    after = []
    for t in range(pps):
        suffix = (sums[(3 * t) * n_heads:(3 * t + 1) * n_heads]
                  + sums[(3 * t + 1) * n_heads:(3 * t + 2) * n_heads]
                  + sums[(3 * t + 2) * n_heads:(3 * t + 3) * n_heads])
        after.append(suffix + carry)
        carry = carry + (suffix[:, 0:1] + lf_pages[t][:, 0:1])
    carry_ref[...] = carry
    for kh in range(FOX_KV_HEADS):
        q = q_ref[0, kh]
        s = jnp.concatenate(
            [_dot_nt(q, _kv_rows(kv_pages[t], kh, FOX_KV_HEADS).astype(bf16))
             + jnp.concatenate([after[t][kh * group:(kh + 1) * group]] * s_new, axis=0)
             for t in range(pps)], axis=1)
        vs = [_kv_rows(kv_pages[t], FOX_KV_HEADS + kh, FOX_KV_HEADS).astype(bf16)
              for t in range(pps)]
        _softmax_step(s + fn_ref[0, kh], vs, m_ref, l_ref, acc_ref, (kh,))

    @pl.when(c == pl.num_programs(1) - 1)
    def _():
        mask = _new_key_mask(rows, group, s_new)
        for kh in range(FOX_KV_HEADS):
            kb = new_ref[0, :, kh * d:(kh + 1) * d].astype(bf16)
            vb = new_ref[0, :, kvw + kh * d:kvw + (kh + 1) * d].astype(bf16)
            s = _dot_nt(q_ref[0, kh], kb) + (fn_ref[0, kh] - fnl_ref[0, kh])
            _softmax_step(jnp.where(mask, s, NEG_INF), vb, m_ref, l_ref, acc_ref, (kh,))
            o_ref[0, kh] = (acc_ref[kh] / l_ref[kh]).astype(o_ref.dtype)


def _fox_sample(page_table, q, fn, fnl, kv_new, cache_kv, cache_logf, *, group, s_new):
    n, kh, r, d = q.shape
    n_pages = page_table.shape[1]
    pps = _pages_per_step(n_pages)
    n_heads = kh * group
    return pl.pallas_call(
        functools.partial(_fox_sample_kernel, pps=pps, group=group, s_new=s_new),
        grid_spec=pltpu.PrefetchScalarGridSpec(
            num_scalar_prefetch=1,
            grid=(n, n_pages // pps),
            in_specs=[_per_seq(q.shape), _per_seq(fn.shape), _per_seq(fnl.shape),
                      _per_seq(kv_new.shape)]
            + _page_specs(PAGE_SIZE * 2 * kh, n_pages, pps, True)
            + _page_specs(n_heads, n_pages, pps, True),
            out_specs=_per_seq(q.shape),
            scratch_shapes=[pltpu.VMEM((kh, r, 1), f32), pltpu.VMEM((kh, r, 1), f32),
                            pltpu.VMEM((kh, r, d), f32), pltpu.VMEM((n_heads, LANES), f32)]),
        out_shape=jax.ShapeDtypeStruct(q.shape, bf16),
        compiler_params=_params("parallel", "arbitrary"),
    )(page_table, q, fn, fnl, kv_new, *([cache_kv] * pps), *([cache_logf] * pps))


def _t5_bucket(dist):
    n = jnp.maximum(dist, 0)
    exact = RPE_BUCKETS // 2
    nf = jnp.maximum(n, 1).astype(f32)
    log_b = exact + (jnp.log(nf / exact) / math.log(RPE_MAX_DIST / exact)
                     * (RPE_BUCKETS - exact)).astype(jnp.int32)
    return jnp.where(n < exact, n, jnp.minimum(log_b, RPE_BUCKETS - 1))


def _rope_tables(pos):
    inv = ROPE_THETA ** (-jnp.arange(0, MLA_ROPE, 2, dtype=f32) / MLA_ROPE)
    ang = pos.astype(f32)[:, None] * inv[None, :]
    cos, sin = jnp.cos(ang), jnp.sin(ang)
    z = jnp.zeros((pos.shape[0], LANES - MLA_ROPE), f32)
    return jnp.concatenate([cos, cos, z], axis=1), jnp.concatenate([-sin, sin, z], axis=1)


def _pad_cols(w, width):
    return jnp.pad(w, ((0, 0), (0, width - w.shape[1])))


def _pad_rows(a, rows):
    return jnp.pad(a, ((0, 0), (0, rows - a.shape[1]), (0, 0)))


def _to_kv_rows(a, n, s, kvh, group):
    return a.reshape(n, s, kvh, group, HEAD_DIM).transpose(0, 2, 1, 3, 4).reshape(
        n, kvh, s * group, HEAD_DIM)


def _from_kv_rows(a, n, s, kvh, group):
    return a.reshape(n, kvh, s, group, HEAD_DIM).transpose(0, 2, 1, 3, 4).reshape(
        n * s, kvh * group * HEAD_DIM)


def kernel(x_prompt, x_sample, c_prompt, c_sample, cache_moba_kv, cache_mla, cache_fox_kv,
           cache_fox_logf, page_table, rpe_table, norm_mix_g, norm_ffn_g, w_ada, b_ada,
           w_in_even, moba_q_g, moba_k_g, mla_q_a_g, w_uq, mla_kv_a_g, w_ukv, mla_q_g, mla_k_g,
           w_out_even, w_in_odd, fox_f_b, fox_q_g, fox_k_g, w_out_odd, w_gate_up, w_down):
    nb_, t, dm = x_prompt.shape
    n, s_new, _ = x_sample.shape
    n_pages = page_table.shape[1]
    past = n_pages * PAGE_SIZE
    assert t % MOBA_BLOCK == 0 and n_pages % 2 == 0 and MOBA_BLOCK >= RPE_MAX_DIST
    moba_heads = dm // (2 * HEAD_DIM)
    mla_heads = dm // (2 * HEAD_DIM)
    fox_heads = dm // HEAD_DIM
    q_lora, kv_lora = dm // 4, dm // 8
    moba_group = moba_heads // MOBA_KV_HEADS
    fox_group = fox_heads // FOX_KV_HEADS
    n_moba_blocks = past // MOBA_BLOCK
    tm = 512

    w_in_even_p = _pad_cols(w_in_even[0], w_in_even.shape[2] + LANES - MLA_ROPE).astype(bf16)
    w_uq_p = jnp.pad(w_uq[0].reshape(q_lora, mla_heads, MLA_QK),
                     ((0, 0), (0, 0), (0, 2 * LANES - MLA_QK))).reshape(q_lora, -1).astype(bf16)
    w_ukv_b = w_ukv[0].astype(bf16)
    w_ukv_h = w_ukv[0].reshape(kv_lora, mla_heads, MLA_NOPE + MLA_V)
    w_uk_t = w_ukv_h[:, :, :MLA_NOPE].reshape(kv_lora, -1).T.astype(bf16)
    w_uv = w_ukv_h[:, :, MLA_NOPE:].reshape(kv_lora, -1).astype(bf16)
    w_in_odd_p = _pad_cols(w_in_odd[0], w_in_odd.shape[2] + LANES - fox_heads).astype(bf16)
    fox_b_p = _pad_cols(fox_f_b[0][None], LANES)
    qk_gain = (jnp.concatenate([mla_q_g[0], mla_q_g[0][MLA_NOPE:]])
               * jnp.concatenate([mla_k_g[0], mla_k_g[0][MLA_NOPE:]]) * MLA_SCALE)
    qk_gain = _pad_cols(qk_gain[None], 2 * LANES)
    even_gains = (moba_q_g[0][None], moba_k_g[0][None], mla_q_a_g[0][None], mla_kv_a_g[0][None])

    i = jnp.arange(MOBA_BLOCK)
    dist = i[:, None] - i[None, :]
    to_heads = lambda b: b.transpose(2, 0, 1)
    b_own = to_heads(rpe_table[_t5_bucket(dist)])
    b_prev = to_heads(rpe_table[_t5_bucket(dist + MOBA_BLOCK)])
    far = rpe_table[_t5_bucket(jnp.array(MOBA_BLOCK + 1))]
    b_far = jnp.broadcast_to(far[:, None, None], (moba_heads, 1, MOBA_BLOCK))
    s_idx = jnp.arange(s_new)
    chunk_keys = _pages_per_step(n_pages) * PAGE_SIZE
    assert chunk_keys % MOBA_BLOCK == 0
    d_last = (past + s_idx)[:, None] - (past - chunk_keys + jnp.arange(chunk_keys))[None, :]
    sb_last = rpe_table[_t5_bucket(d_last)].transpose(2, 0, 1)
    d_own = s_idx[:, None] - jnp.arange(LANES)[None, :]
    sb_own = rpe_table[_t5_bucket(d_own)].transpose(2, 0, 1)
    rows_sg = lambda b: b.reshape(MOBA_KV_HEADS, moba_group, s_new, -1).transpose(0, 2, 1, 3).reshape(
        MOBA_KV_HEADS, s_new * moba_group, -1)
    sb_last, sb_own = rows_sg(sb_last), rows_sg(sb_own)
    sb_far = jnp.broadcast_to(rows_sg(jnp.broadcast_to(far[:, None, None], (moba_heads, s_new, 1))),
                              (MOBA_KV_HEADS, s_new * moba_group, chunk_keys))

    cos_p, sin_p = _rope_tables(jnp.arange(t))
    cos_s, sin_s = _rope_tables(past + jnp.arange(s_new))
    cos_s, sin_s = jnp.tile(cos_s, (n, 1)), jnp.tile(sin_s, (n, 1))

    caches = dict(
        moba=cache_moba_kv.reshape(cache_moba_kv.shape[0], -1, PAGE_SIZE * 4, HEAD_DIM),
        mla=cache_mla.transpose(0, 1, 3, 2),
        fox=cache_fox_kv.reshape(cache_fox_kv.shape[0], -1, PAGE_SIZE * 4, HEAD_DIM),
        logf=cache_fox_logf.transpose(0, 1, 3, 2))

    c_all = jnp.concatenate([c_prompt, c_sample], axis=0)

    def modulation(layer):
        mod = _matmul(c_all, w_ada[layer].astype(bf16), out_dtype=f32, tm=c_all.shape[0], tn=1024,
                      bias=b_ada[layer][None], silu_in=True)
        parts = jnp.split(mod, 6, axis=-1)
        prompt = [p[:nb_].reshape(nb_, 1, dm) for p in parts]
        sample = [jnp.repeat(p[nb_:], s_new, axis=0).reshape(1, n * s_new, dm) for p in parts]
        return prompt, sample

    def ffn(x, layer, sc, sh, gate):
        h = _norm_mod(x, norm_ffn_g[layer][None], sc, sh, tm)
        a = _matmul(h, w_gate_up[layer].astype(bf16), out_dtype=bf16, tm=1024, tn=512, swiglu=True)
        d_ff = a.shape[1]
        tk = d_ff // 2 if (d_ff // 2) % LANES == 0 else d_ff
        return _matmul(a, w_down[layer].astype(bf16), out_dtype=f32, tm=1024, tn=512, tk=tk,
                       res=x, gate=gate)

    def even_front(x, sc, sh, cos_t, sin_t, with_kmean):
        h = _norm_mod(x, norm_mix_g[0][None], sc, sh, tm)
        z = _matmul(h, w_in_even_p, out_dtype=f32, tm=tm, tn=w_in_even_p.shape[1])
        qa, kva, kb, vb, cqn, rows, ckvn, kpe, *kmean = _even_post(
            z, cos_t, sin_t, even_gains, tm=tm, n_heads=moba_heads, q_lora=q_lora, kv_lora=kv_lora,
            with_kmean=with_kmean)
        qb = _mla_q(cqn, w_uq_p, cos_t, sin_t, qk_gain, tm=tm, n_heads=mla_heads)
        return qa, kva, kb, vb, rows, ckvn, kpe, kmean, qb

    def odd_front(x, sc, sh, group_rows):
        h = _norm_mod(x, norm_mix_g[1][None], sc, sh, tm)
        z = _matmul(h, w_in_odd_p, out_dtype=f32, tm=tm, tn=w_in_odd_p.shape[1])
        return _odd_post(z, fox_q_g[0][None], fox_k_g[0][None], fox_b_p, tm=tm, n_heads=fox_heads,
                         group=group_rows)

    (sh1, sc1, g1, sh2, sc2, g2), mods_s0 = modulation(0)
    x = x_prompt.reshape(nb_ * t, dm)
    qa, kva, kb, vb, rows, ckvn, kpe, kmean, qb = even_front(x, sc1, sh1, cos_p, sin_p, True)
    seq = lambda a: a.reshape(nb_, t, -1)
    o_a = _moba_prompt(seq(qa), seq(kb), seq(vb), kmean[0].reshape(nb_, t // MOBA_BLOCK, -1),
                       b_own, b_prev, b_far)
    k_b, v_b = _mla_kv(ckvn, kpe, w_ukv_b, tm=tm, n_heads=mla_heads)
    o_b = _mla_prompt(seq(qb), seq(k_b), seq(v_b), n_heads=mla_heads, tq=512)
    o = jnp.concatenate([o_a, o_b], axis=-1).reshape(nb_ * t, -1)
    x = _matmul(o, w_out_even[0].astype(bf16), out_dtype=f32, tm=1024, tn=512, res=x, gate=g1)
    x = ffn(x, 0, sc2, sh2, g2)
    new_moba_p = kva.reshape(1, nb_, t, 2, MOBA_KV_HEADS, HEAD_DIM)
    new_mla_p = rows.reshape(1, nb_, t, -1)

    (sh1, sc1, g1, sh2, sc2, g2), mods_s1 = modulation(1)
    q, kv, kb, vb, logf, cum = odd_front(x, sc1, sh1, t)
    tq = min(512, t)
    cum_h = cum.reshape(nb_, t, FOX_KV_HEADS, fox_group)
    fq = cum_h.transpose(0, 2, 1, 3)
    fk = cum_h.reshape(nb_, t // tq, tq, FOX_KV_HEADS, fox_group).transpose(0, 3, 1, 4, 2)
    o = _fox_prompt(seq(q), seq(kb), seq(vb), fq, fk, tq=tq).reshape(nb_ * t, -1)
    x = _matmul(o, w_out_odd[0].astype(bf16), out_dtype=f32, tm=1024, tn=512, res=x, gate=g1)
    y_prompt = ffn(x, 1, sc2, sh2, g2).reshape(nb_, t, dm)
    new_fox_p = kv.reshape(1, nb_, t, 2, FOX_KV_HEADS, HEAD_DIM)
    new_logf_p = logf.reshape(1, nb_, t, fox_heads)

    sh1, sc1, g1, sh2, sc2, g2 = mods_s0
    x = x_sample.reshape(n * s_new, dm)
    qa, kva, kb, vb, rows, ckvn, kpe, _, qb = even_front(x, sc1, sh1, cos_s, sin_s, False)
    qa_r = _to_kv_rows(qa, n, s_new, MOBA_KV_HEADS, moba_group)
    kva_pad = _pad_rows(kva.reshape(n, s_new, -1), PAGE_SIZE)
    o_a = _moba_sample(page_table, qa_r, sb_last, sb_far, sb_own, kva_pad, caches["moba"],
                       group=moba_group, s_new=s_new)
    o_a = _from_kv_rows(o_a, n, s_new, MOBA_KV_HEADS, moba_group)
    q_lat = _mla_absorb(qb, w_uk_t, mla_heads).reshape(n, s_new * mla_heads, kv_lora)
    q_rope = qb.reshape(n, s_new * mla_heads, 2 * LANES)[:, :, MLA_NOPE:MLA_QK]
    rows_pad = _pad_rows(rows.reshape(n, s_new, -1), PAGE_SIZE).transpose(0, 2, 1)
    o_b = _mla_sample(page_table, q_lat, q_rope, w_uk_t, w_uv, rows_pad, caches["mla"],
                      n_heads=mla_heads, s_new=s_new).reshape(n * s_new, -1)
    o = jnp.concatenate([o_a, o_b], axis=-1)
    x = _matmul(o, w_out_even[0].astype(bf16), out_dtype=f32, tm=1024, tn=512, res=x, gate=g1)
    x = ffn(x, 0, sc2, sh2, g2)
    new_moba_s = kva.reshape(1, n, s_new, 2, MOBA_KV_HEADS, HEAD_DIM)
    new_mla_s = rows.reshape(1, n, s_new, -1)

    sh1, sc1, g1, sh2, sc2, g2 = mods_s1
    q, kv, kb, vb, logf, cum = odd_front(x, sc1, sh1, s_new)
    q_r = _to_kv_rows(q, n, s_new, FOX_KV_HEADS, fox_group)
    cum_s = cum.reshape(n, s_new, FOX_KV_HEADS, fox_group)
    fn = cum_s.transpose(0, 2, 1, 3).reshape(n, FOX_KV_HEADS, s_new * fox_group, 1)
    fnl = jnp.broadcast_to(cum_s.transpose(0, 2, 3, 1)[:, :, None],
                           (n, FOX_KV_HEADS, s_new, fox_group, s_new)).reshape(
        n, FOX_KV_HEADS, s_new * fox_group, s_new)
    fnl = jnp.pad(fnl, ((0, 0), (0, 0), (0, 0), (0, LANES - s_new)))
    kv_pad = _pad_rows(kv.reshape(n, s_new, -1), PAGE_SIZE)
    o = _fox_sample(page_table, q_r, fn, fnl, kv_pad, caches["fox"], caches["logf"],
                    group=fox_group, s_new=s_new)
    o = _from_kv_rows(o, n, s_new, FOX_KV_HEADS, fox_group)
    x = _matmul(o, w_out_odd[0].astype(bf16), out_dtype=f32, tm=1024, tn=512, res=x, gate=g1)
    y_sample = ffn(x, 1, sc2, sh2, g2).reshape(n, s_new, dm)
    new_fox_s = kv.reshape(1, n, s_new, 2, FOX_KV_HEADS, HEAD_DIM)
    new_logf_s = logf.reshape(1, n, s_new, fox_heads)

    return (y_prompt, y_sample, new_moba_p, new_moba_s, new_mla_p, new_mla_s,
            new_fox_p, new_fox_s, new_logf_p, new_logf_s)
```

```python
import functools
import math

import numpy as np
import jax
import jax.numpy as jnp
from jax import lax
from jax.experimental import pallas as pl
from jax.experimental.pallas import tpu as pltpu

f32 = jnp.float32
bf16 = jnp.bfloat16

HEAD_DIM = 128
MOBA_KV_HEADS = 2
MOBA_BLOCK = 256
MOBA_TOPK = 3
MLA_NOPE = 128
MLA_ROPE = 64
MLA_V = 128
MLA_QK = MLA_NOPE + MLA_ROPE
ROPE_THETA = 10000.0
FOX_KV_HEADS = 2
RPE_BUCKETS = 32
RPE_MAX_DIST = 128
PAGE_SIZE = 128
EPS = 1e-6
NEG_INF = -1e30
ATTN_SCALE = HEAD_DIM ** -0.5
MLA_SCALE = MLA_QK ** -0.5

LANES = 128
VMEM_LIMIT = 56 * 1024 * 1024
PAGES_PER_STEP = 16
MLA_KEY_STEP = 512

_NT = (((1,), (1,)), ((), ()))


def _dot(a, b):
    return jnp.dot(a, b, preferred_element_type=f32)


def _dot_nt(a, b):
    return lax.dot_general(a, b, _NT, preferred_element_type=f32)


def _split3(x):
    hi = x.astype(bf16)
    r1 = x - hi.astype(f32)
    mid = r1.astype(bf16)
    lo = (r1 - mid.astype(f32)).astype(bf16)
    return hi, mid, lo


def _rms(x, g):
    return x * lax.rsqrt(jnp.mean(x * x, axis=-1, keepdims=True) + EPS) * g


def _params(*sem):
    return pltpu.CompilerParams(dimension_semantics=sem, vmem_limit_bytes=VMEM_LIMIT)


def _mm_kernel(*refs, nk, swiglu, has_bias, has_res, silu_in):
    it = iter(refs)
    a_ref = next(it)
    w_ref = next(it)
    w2_ref = next(it) if swiglu else None
    b_ref = next(it) if has_bias else None
    x_ref = next(it) if has_res else None
    g_ref = next(it) if has_res else None
    o_ref = next(it)
    accs = list(it)

    a = a_ref[...]
    if silu_in:
        a = a * jax.nn.sigmoid(a)
    a = a.astype(bf16)

    def finish(acc, acc2):
        r = acc
        if swiglu:
            r = (acc * jax.nn.sigmoid(acc)) * acc2
        if has_bias:
            r = r + b_ref[...]
        if has_res:
            r = x_ref[...] + g_ref[0] * r
        o_ref[...] = r.astype(o_ref.dtype)

    if nk == 1:
        finish(_dot(a, w_ref[...]), _dot(a, w2_ref[...]) if swiglu else None)
        return

    k = pl.program_id(2)

    @pl.when(k == 0)
    def _():
        for acc in accs:
            acc[...] = jnp.zeros_like(acc)

    accs[0][...] += _dot(a, w_ref[...])
    if swiglu:
        accs[1][...] += _dot(a, w2_ref[...])

    @pl.when(k == nk - 1)
    def _():
        finish(accs[0][...], accs[1][...] if swiglu else None)


def _matmul(a, w, *, out_dtype, tm, tn, tk=None, swiglu=False, bias=None, res=None,
            gate=None, silu_in=False):
    m, kd = a.shape
    n = w.shape[1] // 2 if swiglu else w.shape[1]
    tm = min(tm, m)
    tn = min(tn, n)
    tk = kd if tk is None else tk
    assert m % tm == 0 and n % tn == 0 and kd % tk == 0
    nk = kd // tk
    nj = n // tn
    in_specs = [pl.BlockSpec((tm, tk), lambda i, j, k: (i, k)),
                pl.BlockSpec((tk, tn), lambda i, j, k: (k, j))]
    args = [a, w]
    if swiglu:
        in_specs.append(pl.BlockSpec((tk, tn), lambda i, j, k: (k, j + nj)))
        args.append(w)
    if bias is not None:
        in_specs.append(pl.BlockSpec((1, tn), lambda i, j, k: (0, j)))
        args.append(bias)
    if res is not None:
        tiles_per_group = (m // tm) // gate.shape[0]
        in_specs.append(pl.BlockSpec((tm, tn), lambda i, j, k: (i, j)))
        in_specs.append(pl.BlockSpec((1, gate.shape[1], tn),
                                     lambda i, j, k: (i // tiles_per_group, 0, j)))
        args += [res, gate]
    scratch = []
    if nk > 1:
        scratch = [pltpu.VMEM((tm, tn), f32)] * (2 if swiglu else 1)
    return pl.pallas_call(
        functools.partial(_mm_kernel, nk=nk, swiglu=swiglu, has_bias=bias is not None,
                          has_res=res is not None, silu_in=silu_in),
        grid=(m // tm, nj, nk),
        in_specs=in_specs,
        out_specs=pl.BlockSpec((tm, tn), lambda i, j, k: (i, j)),
        out_shape=jax.ShapeDtypeStruct((m, n), out_dtype),
        scratch_shapes=scratch,
        compiler_params=_params("parallel", "parallel", "arbitrary"),
    )(*args)


def _norm_mod_kernel(x_ref, g_ref, sc_ref, sh_ref, o_ref):
    y = _rms(x_ref[...], g_ref[...])
    o_ref[...] = (y * (1.0 + sc_ref[0]) + sh_ref[0]).astype(o_ref.dtype)


def _norm_mod(x, g, sc, sh, tm):
    m, d = x.shape
    tm = min(tm, m)
    tiles_per_group = (m // tm) // sc.shape[0]
    mod_spec = pl.BlockSpec((1, sc.shape[1], d), lambda i: (i // tiles_per_group, 0, 0))
    return pl.pallas_call(
        _norm_mod_kernel,
        grid=(m // tm,),
        in_specs=[pl.BlockSpec((tm, d), lambda i: (i, 0)),
                  pl.BlockSpec((1, d), lambda i: (0, 0)), mod_spec, mod_spec],
        out_specs=pl.BlockSpec((tm, d), lambda i: (i, 0)),
        out_shape=jax.ShapeDtypeStruct((m, d), bf16),
        compiler_params=_params("parallel"),
    )(x, g, sc, sh)


def _rope128(x, cos_ref, sin_ref):
    lane = lax.broadcasted_iota(jnp.int32, x.shape, 1)
    swapped = jnp.where(lane < MLA_ROPE // 2, pltpu.roll(x, LANES - MLA_ROPE // 2, 1),
                        pltpu.roll(x, MLA_ROPE // 2, 1))
    return x * cos_ref[...] + swapped * sin_ref[...]


def _store_with_ones(vb_ref, v, n_heads):
    d = HEAD_DIM
    for h in range(n_heads):
        vb_ref[:, 2 * h * d:(2 * h + 1) * d] = v[:, h * d:(h + 1) * d].astype(bf16)
        vb_ref[:, (2 * h + 1) * d:(2 * h + 2) * d] = jnp.ones((v.shape[0], d), bf16)


def _even_post_kernel(z_ref, cos_ref, sin_ref, qg_ref, kg_ref, cqg_ref, ckvg_ref,
                      qa_ref, kva_ref, kb_ref, vb_ref, cqn_ref, rows_ref, ckvn_ref, kpe_ref,
                      *maybe_kmean_ref, n_heads, q_lora, kv_lora):
    tm = z_ref.shape[0]
    d = HEAD_DIM
    for h in range(n_heads):
        qa_ref[:, h * d:(h + 1) * d] = _rms(z_ref[:, h * d:(h + 1) * d], qg_ref[...])
    off = n_heads * d
    for h in range(MOBA_KV_HEADS):
        k = _rms(z_ref[:, off + h * d:off + (h + 1) * d], kg_ref[...])
        kva_ref[:, h * d:(h + 1) * d] = k
        kb_ref[:, h * d:(h + 1) * d] = k.astype(bf16)
        for blk in range(tm // MOBA_BLOCK if maybe_kmean_ref else 0):
            maybe_kmean_ref[0][blk, :, h * d:(h + 1) * d] = jnp.mean(
                k[blk * MOBA_BLOCK:(blk + 1) * MOBA_BLOCK], axis=0, keepdims=True)
    off += MOBA_KV_HEADS * d
    v = z_ref[:, off:off + MOBA_KV_HEADS * d]
    kva_ref[:, MOBA_KV_HEADS * d:] = v
    _store_with_ones(vb_ref, v, MOBA_KV_HEADS)
    off += MOBA_KV_HEADS * d
    cqn_ref[...] = _rms(z_ref[:, off:off + q_lora], cqg_ref[...]).astype(bf16)
    off += q_lora
    ckvn = _rms(z_ref[:, off:off + kv_lora], ckvg_ref[...])
    rows_ref[:, :kv_lora] = ckvn
    ckvn_ref[...] = ckvn.astype(bf16)
    off += kv_lora
    kr = _rope128(z_ref[:, off:off + LANES], cos_ref, sin_ref)
    kpe_ref[...] = kr
    rows_ref[:, kv_lora:] = kr[:, :MLA_ROPE]


def _even_post(z, cos_t, sin_t, prm_g, *, tm, n_heads, q_lora, kv_lora, with_kmean):
    m = z.shape[0]
    tm = min(tm, m)
    t_tiles = cos_t.shape[0] // tm
    d = HEAD_DIM
    row = lambda w: pl.BlockSpec((tm, w), lambda i: (i, 0))
    vec = lambda w: pl.BlockSpec((1, w), lambda i: (0, 0))
    tab = pl.BlockSpec((tm, LANES), lambda i: (i % t_tiles, 0))
    nblk = tm // MOBA_BLOCK
    kvw = MOBA_KV_HEADS * d
    outs = [
        jax.ShapeDtypeStruct((m, n_heads * d), f32),
        jax.ShapeDtypeStruct((m, 2 * kvw), f32),
        jax.ShapeDtypeStruct((m, kvw), bf16),
        jax.ShapeDtypeStruct((m, 2 * kvw), bf16),
        jax.ShapeDtypeStruct((m, q_lora), bf16),
        jax.ShapeDtypeStruct((m, kv_lora + MLA_ROPE), f32),
        jax.ShapeDtypeStruct((m, kv_lora), bf16),
        jax.ShapeDtypeStruct((m, LANES), f32),
    ]
    out_specs = [row(n_heads * d), row(2 * kvw), row(kvw), row(2 * kvw), row(q_lora),
                 row(kv_lora + MLA_ROPE), row(kv_lora), row(LANES)]
    if with_kmean:
        assert tm % MOBA_BLOCK == 0
        outs.append(jax.ShapeDtypeStruct((m // MOBA_BLOCK, 1, kvw), f32))
        out_specs.append(pl.BlockSpec((nblk, 1, kvw), lambda i: (i, 0, 0)))
    return pl.pallas_call(
        functools.partial(_even_post_kernel, n_heads=n_heads, q_lora=q_lora, kv_lora=kv_lora),
        grid=(m // tm,),
        in_specs=[row(z.shape[1]), tab, tab, vec(d), vec(d), vec(q_lora), vec(kv_lora)],
        out_specs=out_specs,
        out_shape=outs,
        compiler_params=_params("parallel"),
    )(z, cos_t, sin_t, *prm_g)


def _mla_q_kernel(c_ref, w_ref, cos_ref, sin_ref, g_ref, o_ref):
    acc = _dot(c_ref[...], w_ref[...])
    nope = acc[:, :MLA_NOPE]
    rope = _rope128(acc[:, MLA_NOPE:], cos_ref, sin_ref)
    ss = jnp.sum(nope * nope, axis=-1, keepdims=True) + jnp.sum(rope * rope, axis=-1, keepdims=True)
    inv = lax.rsqrt(ss * (1.0 / MLA_QK) + EPS)
    o_ref[:, :MLA_NOPE] = (nope * inv * g_ref[:, :MLA_NOPE]).astype(o_ref.dtype)
    o_ref[:, MLA_NOPE:] = (rope * inv * g_ref[:, MLA_NOPE:]).astype(o_ref.dtype)


def _mla_q(cqn, w_uq_pad, cos_t, sin_t, gain, *, tm, n_heads):
    m, ql = cqn.shape
    tm = min(tm, m)
    t_tiles = cos_t.shape[0] // tm
    tab = pl.BlockSpec((tm, LANES), lambda i, h: (i % t_tiles, 0))
    return pl.pallas_call(
        _mla_q_kernel,
        grid=(m // tm, n_heads),
        in_specs=[pl.BlockSpec((tm, ql), lambda i, h: (i, 0)),
                  pl.BlockSpec((ql, 2 * LANES), lambda i, h: (0, h)), tab, tab,
                  pl.BlockSpec((1, 2 * LANES), lambda i, h: (0, 0))],
        out_specs=pl.BlockSpec((tm, 2 * LANES), lambda i, h: (i, h)),
        out_shape=jax.ShapeDtypeStruct((m, n_heads * 2 * LANES), bf16),
        compiler_params=_params("parallel", "arbitrary"),
    )(cqn, w_uq_pad, cos_t, sin_t, gain)


def _mla_kv_kernel(c_ref, kpe_ref, w_ref, k_ref, v_ref):
    acc = _dot(c_ref[...], w_ref[...])
    kn = acc[:, :MLA_NOPE]
    kp = kpe_ref[...]
    ss = jnp.sum(kn * kn, axis=-1, keepdims=True) + jnp.sum(kp * kp, axis=-1, keepdims=True)
    inv = lax.rsqrt(ss * (1.0 / MLA_QK) + EPS)
    k_ref[:, :MLA_NOPE] = (kn * inv).astype(bf16)
    k_ref[:, MLA_NOPE:] = (kp * inv).astype(bf16)
    v_ref[:, :MLA_V] = acc[:, MLA_NOPE:].astype(bf16)
    v_ref[:, MLA_V:] = jnp.ones((acc.shape[0], LANES), bf16)


def _mla_kv(ckvn, kpe, w_ukv, *, tm, n_heads):
    m, kl = ckvn.shape
    tm = min(tm, m)
    return pl.pallas_call(
        _mla_kv_kernel,
        grid=(m // tm, n_heads),
        in_specs=[pl.BlockSpec((tm, kl), lambda i, h: (i, 0)),
                  pl.BlockSpec((tm, LANES), lambda i, h: (i, 0)),
                  pl.BlockSpec((kl, 2 * LANES), lambda i, h: (0, h))],
        out_specs=[pl.BlockSpec((tm, 2 * LANES), lambda i, h: (i, h)),
                   pl.BlockSpec((tm, MLA_V + LANES), lambda i, h: (i, h))],
        out_shape=[jax.ShapeDtypeStruct((m, n_heads * 2 * LANES), bf16),
                   jax.ShapeDtypeStruct((m, n_heads * (MLA_V + LANES)), bf16)],
        compiler_params=_params("parallel", "arbitrary"),
    )(ckvn, kpe, w_ukv)


def _odd_post_kernel(z_ref, qg_ref, kg_ref, fb_ref, q_ref, kv_ref, kb_ref, vb_ref,
                     logf_ref, cum_ref, carry_ref, *, n_heads, group, tiles_per_seq):
    tm = z_ref.shape[0]
    d = HEAD_DIM
    for h in range(n_heads):
        q_ref[:, h * d:(h + 1) * d] = (_rms(z_ref[:, h * d:(h + 1) * d], qg_ref[...])
                                       * ATTN_SCALE).astype(q_ref.dtype)
    off = n_heads * d
    for h in range(FOX_KV_HEADS):
        k = _rms(z_ref[:, off + h * d:off + (h + 1) * d], kg_ref[...])
        kv_ref[:, h * d:(h + 1) * d] = k
        kb_ref[:, h * d:(h + 1) * d] = k.astype(bf16)
    off += FOX_KV_HEADS * d
    v = z_ref[:, off:off + FOX_KV_HEADS * d]
    kv_ref[:, FOX_KV_HEADS * d:] = v
    _store_with_ones(vb_ref, v, FOX_KV_HEADS)
    off += FOX_KV_HEADS * d
    x = z_ref[:, off:off + LANES] + fb_ref[...]
    logf = jnp.minimum(x, 0.0) - jnp.log1p(jnp.exp(-jnp.abs(x)))
    logf_ref[...] = logf[:, :n_heads]

    r = lax.broadcasted_iota(jnp.int32, (tm, tm), 0)
    c = lax.broadcasted_iota(jnp.int32, (tm, tm), 1)
    tri = (c <= r) if group >= tm else ((c <= r) & (c // group == r // group))
    tri = jnp.where(tri, 1.0, 0.0).astype(bf16)
    hi, mid, lo = _split3(logf)
    cum = _dot(tri, hi) + _dot(tri, mid) + _dot(tri, lo)
    if group >= tm:
        first = pl.program_id(0) % tiles_per_seq == 0

        @pl.when(first)
        def _():
            carry_ref[...] = jnp.zeros_like(carry_ref)

        cum = cum + carry_ref[...]
        carry_ref[...] = cum[tm - 1:tm, :]
    cum_ref[...] = cum[:, :n_heads]


def _odd_post(z, qg, kg, fb_pad, *, tm, n_heads, group):
    m = z.shape[0]
    tm = min(tm, m)
    assert group >= tm and group % tm == 0 or tm % group == 0
    d = HEAD_DIM
    kvw = FOX_KV_HEADS * d
    row = lambda w: pl.BlockSpec((tm, w), lambda i: (i, 0))
    vec = lambda w: pl.BlockSpec((1, w), lambda i: (0, 0))
    outs = [jax.ShapeDtypeStruct((m, n_heads * d), bf16),
            jax.ShapeDtypeStruct((m, 2 * kvw), f32),
            jax.ShapeDtypeStruct((m, kvw), bf16),
            jax.ShapeDtypeStruct((m, 2 * kvw), bf16),
            jax.ShapeDtypeStruct((m, n_heads), f32),
            jax.ShapeDtypeStruct((m, n_heads), f32)]
    return pl.pallas_call(
        functools.partial(_odd_post_kernel, n_heads=n_heads, group=group,
                          tiles_per_seq=max(group // tm, 1)),
        grid=(m // tm,),
        in_specs=[row(z.shape[1]), vec(d), vec(d), vec(LANES)],
        out_specs=[row(n_heads * d), row(2 * kvw), row(kvw), row(2 * kvw), row(n_heads), row(n_heads)],
        out_shape=outs,
        scratch_shapes=[pltpu.VMEM((1, LANES), f32)],
        compiler_params=_params("arbitrary"),
    )(z, qg, kg, fb_pad)


def _softmax_init(m_ref, l_ref, acc_ref):
    m_ref[...] = jnp.full_like(m_ref, NEG_INF)
    if l_ref is not None:
        l_ref[...] = jnp.zeros_like(l_ref)
    acc_ref[...] = jnp.zeros_like(acc_ref)


def _normalised(acc, dv):
    return acc[:, :dv] / acc[:, dv:dv + LANES][:, :dv]


def _softmax_step(s, v, m_ref, l_ref, acc_ref, idx=(), nt=False):
    idx = idx if idx else Ellipsis
    m_prev = m_ref[idx]
    m_new = jnp.maximum(m_prev, jnp.max(s, axis=-1, keepdims=True))
    alpha = jnp.exp(m_prev - m_new)
    p = jnp.exp(s - m_new)
    if l_ref is not None:
        l_ref[idx] = alpha * l_ref[idx] + jnp.sum(p, axis=-1, keepdims=True)
    pb = p.astype(bf16)
    if isinstance(v, (list, tuple)):
        w = s.shape[1] // len(v)
        pv = _dot(pb[:, :w], v[0])
        for t in range(1, len(v)):
            pv = pv + _dot(pb[:, t * w:(t + 1) * w], v[t])
    else:
        pv = _dot_nt(pb, v) if nt else _dot(pb, v)
    acc_ref[idx] = alpha * acc_ref[idx] + pv
    m_ref[idx] = m_new


def _with_ones(v):
    return jnp.concatenate([v, jnp.ones((v.shape[0], LANES), v.dtype)], axis=1)


def _causal(tq, tk, q0, k0):
    r = lax.broadcasted_iota(jnp.int32, (tq, tk), 0) + q0
    c = lax.broadcasted_iota(jnp.int32, (tq, tk), 1) + k0
    return r >= c


def _mla_prompt_kernel(q_ref, k_ref, v_ref, o_ref, m_ref, acc_ref, *, tq):
    i = pl.program_id(2)
    q = q_ref[0]
    _softmax_init(m_ref, None, acc_ref)

    def body(j, carry):
        ks = pl.ds(pl.multiple_of(j * tq, tq), tq)
        _softmax_step(_dot_nt(q, k_ref[0, ks, :]), v_ref[0, ks, :], m_ref, None, acc_ref)
        return carry

    lax.fori_loop(0, i, body, 0)
    ks = pl.ds(pl.multiple_of(i * tq, tq), tq)
    s = jnp.where(_causal(tq, tq, 0, 0), _dot_nt(q, k_ref[0, ks, :]), NEG_INF)
    _softmax_step(s, v_ref[0, ks, :], m_ref, None, acc_ref)
    o_ref[0] = _normalised(acc_ref[...], MLA_V).astype(o_ref.dtype)


def _mla_prompt(q, k, v, *, n_heads, tq):
    b, t, _ = q.shape
    tq = min(tq, t)
    return pl.pallas_call(
        functools.partial(_mla_prompt_kernel, tq=tq),
        grid=(b, n_heads, t // tq),
        in_specs=[pl.BlockSpec((1, tq, 2 * LANES), lambda b_, h, i: (b_, i, h)),
                  pl.BlockSpec((1, t, 2 * LANES), lambda b_, h, i: (b_, 0, h)),
                  pl.BlockSpec((1, t, MLA_V + LANES), lambda b_, h, i: (b_, 0, h))],
        out_specs=pl.BlockSpec((1, tq, MLA_V), lambda b_, h, i: (b_, i, h)),
        out_shape=jax.ShapeDtypeStruct((b, t, n_heads * MLA_V), bf16),
        scratch_shapes=[pltpu.VMEM((tq, 1), f32), pltpu.VMEM((tq, MLA_V + LANES), f32)],
        compiler_params=_params("parallel", "parallel", "arbitrary"),
    )(q, k, v)


def _fox_prompt_kernel(q_ref, k_ref, v_ref, fq_ref, fk_ref, o_ref, m_ref, acc_ref,
                       *, tq, group):
    i = pl.program_id(2)
    d = HEAD_DIM
    _softmax_init(m_ref, None, acc_ref)

    def block(j, diag):
        ks = pl.ds(pl.multiple_of(j * tq, tq), tq)
        kb = k_ref[0, ks, :]
        vb = v_ref[0, ks, :]
        for g in range(group):
            s = _dot_nt(q_ref[0, :, g * d:(g + 1) * d], kb)
            s = s + (fq_ref[0, 0, :, g:g + 1] - fk_ref[0, 0, j, g:g + 1, :])
            if diag:
                s = jnp.where(_causal(tq, tq, 0, 0), s, NEG_INF)
            _softmax_step(s, vb, m_ref, None, acc_ref, (g,))

    def body(j, carry):
        block(j, False)
        return carry

    lax.fori_loop(0, i, body, 0)
    block(i, True)
    for g in range(group):
        o_ref[0, :, g * d:(g + 1) * d] = _normalised(acc_ref[g], d).astype(o_ref.dtype)


def _fox_prompt(q, kb, vb, fq, fk, *, tq):
    b, t, hd = q.shape
    group = hd // HEAD_DIM // FOX_KV_HEADS
    assert fk.shape[-1] == tq
    gd = group * HEAD_DIM
    return pl.pallas_call(
        functools.partial(_fox_prompt_kernel, tq=tq, group=group),
        grid=(b, FOX_KV_HEADS, t // tq),
        in_specs=[pl.BlockSpec((1, tq, gd), lambda b_, h, i: (b_, i, h)),
                  pl.BlockSpec((1, t, HEAD_DIM), lambda b_, h, i: (b_, 0, h)),
                  pl.BlockSpec((1, t, 2 * HEAD_DIM), lambda b_, h, i: (b_, 0, h)),
                  pl.BlockSpec((1, 1, tq, group), lambda b_, h, i: (b_, h, i, 0)),
                  pl.BlockSpec((1, 1, t // tq, group, tq), lambda b_, h, i: (b_, h, 0, 0, 0))],
        out_specs=pl.BlockSpec((1, tq, gd), lambda b_, h, i: (b_, i, h)),
        out_shape=jax.ShapeDtypeStruct((b, t, hd), bf16),
        scratch_shapes=[pltpu.VMEM((group, tq, 1), f32),
                        pltpu.VMEM((group, tq, 2 * HEAD_DIM), f32)],
        compiler_params=_params("parallel", "parallel", "arbitrary"),
    )(q, kb, vb, fq, fk)


def _topk_mask(score, n_valid):
    nb = score.shape[1]
    blk = lax.broadcasted_iota(jnp.int32, score.shape, 1)
    score = jnp.where(blk < n_valid, score, NEG_INF)
    rank = jnp.zeros(score.shape, f32)
    for c in range(nb):
        col = score[:, c:c + 1]
        ahead = jnp.where(col > score, 1.0, jnp.where((col == score) & (blk > c), 1.0, 0.0))
        rank = rank + ahead
    return jnp.where((rank < MOBA_TOPK) & (blk < n_valid), 1.0, 0.0)


def _block_scores(q, km):
    qh, qm, ql = _split3(q)
    kh, kmid, kl = _split3(km)
    return (_dot_nt(qh, kh) + (_dot_nt(qh, kmid) + _dot_nt(qm, kh))
            + (_dot_nt(qh, kl) + _dot_nt(qm, kmid) + _dot_nt(ql, kh)))


def _moba_prompt_kernel(q_ref, k_ref, v_ref, km_ref, bown_ref, bprev_ref, bfar_ref, o_ref,
                        sel_ref, qs_ref, m_ref, acc_ref, *, group):
    i = pl.program_id(2)
    d = HEAD_DIM
    tq = MOBA_BLOCK
    nb = km_ref.shape[1]
    _softmax_init(m_ref, None, acc_ref)
    km = km_ref[0]
    for g in range(group):
        q = q_ref[0, :, g * d:(g + 1) * d]
        sel_ref[g] = _topk_mask(_block_scores(q, km), i)
        qs_ref[g] = (q * ATTN_SCALE).astype(bf16)

    def block(j, kind):
        ks = pl.ds(pl.multiple_of(j * tq, tq), tq)
        kb = k_ref[0, ks, :]
        vb = v_ref[0, ks, :]
        for g in range(group):
            s = _dot_nt(qs_ref[g], kb)
            if kind == "own":
                s = jnp.where(_causal(tq, tq, 0, 0), s + bown_ref[g], NEG_INF)
            else:
                s = s + (bprev_ref[g] if kind == "prev" else bfar_ref[g])
                lane = lax.broadcasted_iota(jnp.int32, (tq, nb), 1)
                picked = jnp.sum(jnp.where(lane == j, sel_ref[g], 0.0), axis=-1, keepdims=True)
                s = jnp.where(picked > 0.5, s, NEG_INF)
            _softmax_step(s, vb, m_ref, None, acc_ref, (g,))

    block(i, "own")

    @pl.when(i >= 1)
    def _():
        block(i - 1, "prev")

    def body(j, carry):
        block(j, "far")
        return carry

    lax.fori_loop(0, i - 1, body, 0)
    for g in range(group):
        o_ref[0, :, g * d:(g + 1) * d] = _normalised(acc_ref[g], d).astype(o_ref.dtype)


def _moba_prompt(qa, kb, vb, kmean, b_own, b_prev, b_far):
    b, t, hd = qa.shape
    group = hd // HEAD_DIM // MOBA_KV_HEADS
    nb = t // MOBA_BLOCK
    gd = group * HEAD_DIM
    blk = MOBA_BLOCK
    bias_spec = pl.BlockSpec((group, blk, blk), lambda b_, h, i: (h, 0, 0))
    return pl.pallas_call(
        functools.partial(_moba_prompt_kernel, group=group),
        grid=(b, MOBA_KV_HEADS, nb),
        in_specs=[pl.BlockSpec((1, blk, gd), lambda b_, h, i: (b_, i, h)),
                  pl.BlockSpec((1, t, HEAD_DIM), lambda b_, h, i: (b_, 0, h)),
                  pl.BlockSpec((1, t, 2 * HEAD_DIM), lambda b_, h, i: (b_, 0, h)),
                  pl.BlockSpec((1, nb, HEAD_DIM), lambda b_, h, i: (b_, 0, h)),
                  bias_spec, bias_spec,
                  pl.BlockSpec((group, 1, blk), lambda b_, h, i: (h, 0, 0))],
        out_specs=pl.BlockSpec((1, blk, gd), lambda b_, h, i: (b_, i, h)),
        out_shape=jax.ShapeDtypeStruct((b, t, hd), bf16),
        scratch_shapes=[pltpu.VMEM((group, blk, nb), f32),
                        pltpu.VMEM((group, blk, HEAD_DIM), bf16),
                        pltpu.VMEM((group, blk, 1), f32),
                        pltpu.VMEM((group, blk, 2 * HEAD_DIM), f32)],
        compiler_params=_params("parallel", "parallel", "arbitrary"),
    )(qa, kb, vb, kmean, b_own, b_prev, b_far)


def _pages_per_step(n_pages):
    return PAGES_PER_STEP if n_pages % PAGES_PER_STEP == 0 else n_pages


def _page_specs(rows, n_pages, pps, reverse):
    def spec(t):
        def index(n, c, pt):
            p = c * pps + t
            return (0, pt[n, n_pages - 1 - p if reverse else p], 0, 0)
        return pl.BlockSpec((None, None, rows, LANES), index)
    return [spec(t) for t in range(pps)]


def _kv_rows(page_ref, which, n_kv_heads):
    return page_ref[pl.ds(which, PAGE_SIZE, stride=2 * n_kv_heads), :]


def _per_seq(shape):
    nd = len(shape)
    return pl.BlockSpec((1,) + tuple(shape[1:]), lambda n, c, pt: (n,) + (0,) * (nd - 1))


def _shared(shape):
    nd = len(shape)
    return pl.BlockSpec(tuple(shape), lambda n, c, pt: (0,) * nd)


def _new_key_mask(rows, group, s_new):
    r = lax.broadcasted_iota(jnp.int32, (rows, LANES), 0) // group
    j = lax.broadcasted_iota(jnp.int32, (rows, LANES), 1)
    return (j <= r) & (j < s_new)


def _moba_sample_kernel(pt_ref, q_ref, blast_ref, bfar_ref, bown_ref, new_ref, *rest,
                        pps, group, s_new):
    pages = rest[:pps]
    o_ref, k_s, v_s, km_s, sel_s, m_ref, l_ref, acc_ref = rest[pps:]
    c = pl.program_id(1)
    nc = pl.num_programs(1)
    d = HEAD_DIM
    kvh = MOBA_KV_HEADS
    rows, nb = sel_s.shape[1], sel_s.shape[2]
    chunk_keys = pps * PAGE_SIZE
    blocks_per_chunk = chunk_keys // MOBA_BLOCK

    for kh in range(kvh):
        sums = []
        for t in range(pps):
            k = _kv_rows(pages[t], kh, kvh)
            v = _kv_rows(pages[t], kvh + kh, kvh)
            r0 = pl.multiple_of((c * pps + t) * PAGE_SIZE, PAGE_SIZE)
            k_s[kh, pl.ds(r0, PAGE_SIZE), :] = k.astype(bf16)
            v_s[kh, pl.ds(r0, PAGE_SIZE), :] = v.astype(bf16)
            sums.append(jnp.sum(k, axis=0, keepdims=True))
        for t in range(0, pps, 2):
            km_s[kh, pl.ds(c * blocks_per_chunk + t // 2, 1), :] = (
                (sums[t] + sums[t + 1]) * (1.0 / MOBA_BLOCK))

    @pl.when(c == nc - 1)
    def _():
        _softmax_init(m_ref, l_ref, acc_ref)
        lane = lax.broadcasted_iota(jnp.int32, (rows, nb), 1)
        for kh in range(kvh):
            sel_s[kh] = _topk_mask(_block_scores(q_ref[0, kh], km_s[kh]), nb)
        qs = [(q_ref[0, kh] * ATTN_SCALE).astype(bf16) for kh in range(kvh)]

        def chunk(ci, carry):
            ks = pl.ds(pl.multiple_of(ci * chunk_keys, chunk_keys), chunk_keys)
            for kh in range(kvh):
                s = _dot_nt(qs[kh], k_s[kh, ks, :])
                bias = jnp.where(ci == nc - 1, blast_ref[kh], bfar_ref[kh])
                sel = sel_s[kh]
                picked = jnp.concatenate(
                    [jnp.broadcast_to(
                        jnp.sum(jnp.where(lane == ci * blocks_per_chunk + b, sel, 0.0),
                                axis=-1, keepdims=True), (rows, MOBA_BLOCK))
                     for b in range(blocks_per_chunk)], axis=1)
                s = jnp.where(picked > 0.5, s + bias, NEG_INF)
                _softmax_step(s, v_s[kh, ks, :], m_ref, l_ref, acc_ref, (kh,))
            return carry

        lax.fori_loop(0, nc, chunk, 0)
        mask = _new_key_mask(rows, group, s_new)
        for kh in range(kvh):
            kb = new_ref[0, :, kh * d:(kh + 1) * d].astype(bf16)
            vb = new_ref[0, :, (kvh + kh) * d:(kvh + kh + 1) * d].astype(bf16)
            s = jnp.where(mask, _dot_nt(qs[kh], kb) + bown_ref[kh], NEG_INF)
            _softmax_step(s, vb, m_ref, l_ref, acc_ref, (kh,))
            o_ref[0, kh] = (acc_ref[kh] / l_ref[kh]).astype(o_ref.dtype)


def _moba_sample(page_table, q, b_last, b_far, b_own, kv_new, cache, *, group, s_new):
    n, kh, r, d = q.shape
    n_pages = page_table.shape[1]
    pps = _pages_per_step(n_pages)
    past = n_pages * PAGE_SIZE
    nb = past // MOBA_BLOCK
    assert pps % 2 == 0
    return pl.pallas_call(
        functools.partial(_moba_sample_kernel, pps=pps, group=group, s_new=s_new),
        grid_spec=pltpu.PrefetchScalarGridSpec(
            num_scalar_prefetch=1,
            grid=(n, n_pages // pps),
            in_specs=[_per_seq(q.shape), _shared(b_last.shape), _shared(b_far.shape),
                      _shared(b_own.shape), _per_seq(kv_new.shape)]
            + _page_specs(PAGE_SIZE * 2 * kh, n_pages, pps, False),
            out_specs=_per_seq(q.shape),
            scratch_shapes=[pltpu.VMEM((kh, past, d), bf16), pltpu.VMEM((kh, past, d), bf16),
                            pltpu.VMEM((kh, nb, d), f32), pltpu.VMEM((kh, r, nb), f32),
                            pltpu.VMEM((kh, r, 1), f32), pltpu.VMEM((kh, r, 1), f32),
                            pltpu.VMEM((kh, r, d), f32)]),
        out_shape=jax.ShapeDtypeStruct(q.shape, bf16),
        compiler_params=_params("arbitrary", "arbitrary"),
    )(page_table, q, b_last, b_far, b_own, kv_new, *([cache] * pps))


def _mla_absorb_kernel(q_ref, w_ref, o_ref):
    o_ref[...] = _dot(q_ref[...], w_ref[...]).astype(o_ref.dtype)


def _mla_absorb(qb, w_uk_t, n_heads):
    m = qb.shape[0]
    kl = w_uk_t.shape[1]
    return pl.pallas_call(
        _mla_absorb_kernel,
        grid=(n_heads,),
        in_specs=[pl.BlockSpec((m, MLA_NOPE), lambda h: (0, 2 * h)),
                  pl.BlockSpec((MLA_NOPE, kl), lambda h: (h, 0))],
        out_specs=pl.BlockSpec((m, kl), lambda h: (0, h)),
        out_shape=jax.ShapeDtypeStruct((m, n_heads * kl), bf16),
        compiler_params=_params("parallel"),
    )(qb, w_uk_t)


def _mla_sample_kernel(pt_ref, ql_ref, qr_ref, wk_ref, wv_ref, new_ref, *rest,
                       pps, n_heads, s_new, kv_lora):
    pages = rest[:pps]
    o_ref, lhs_s, m_ref, l_ref, acc_ref = rest[pps:]
    c = pl.program_id(1)
    rows = s_new * n_heads
    n_up = n_heads * MLA_NOPE

    @pl.when(c == 0)
    def _():
        _softmax_init(m_ref, l_ref, acc_ref)
        lhs_s[:n_up, :] = wk_ref[...]
        lhs_s[n_up:, :] = ql_ref[0]

    def attend(lat_t, kp_t, mask):
        n_keys = lat_t.shape[1]
        step = min(n_keys, MLA_KEY_STEP)
        ssq, s_lat = [], []
        for k0 in range(0, n_keys, step):
            both = _dot(lhs_s[...], lat_t[:, k0:k0 + step])
            ssq.append(jnp.concatenate(
                [jnp.sum(jnp.square(both[h * MLA_NOPE:(h + 1) * MLA_NOPE]), axis=0, keepdims=True)
                 for h in range(n_heads)], axis=0))
            s_lat.append(both[n_up:])
        ssq = jnp.concatenate(ssq, axis=1) + jnp.sum(kp_t * kp_t, axis=0, keepdims=True)
        inv = lax.rsqrt(ssq * (1.0 / MLA_QK) + EPS)
        s = jnp.concatenate(s_lat, axis=1) + _dot(qr_ref[0], kp_t.astype(bf16))
        s = s * jnp.concatenate([inv] * s_new, axis=0)
        if mask is not None:
            s = jnp.where(mask, s, NEG_INF)
        _softmax_step(s, lat_t, m_ref, l_ref, acc_ref, nt=True)

    attend(jnp.concatenate([p[:kv_lora, :].astype(bf16) for p in pages], axis=1),
           jnp.concatenate([p[kv_lora:, :] for p in pages], axis=1), None)

    @pl.when(c == pl.num_programs(1) - 1)
    def _():
        attend(new_ref[0, :kv_lora, :].astype(bf16), new_ref[0, kv_lora:, :],
               _new_key_mask(rows, n_heads, s_new))
        lat = (acc_ref[...] / l_ref[...]).astype(bf16)
        full = _dot(lat, wv_ref[...])
        head = lax.broadcasted_iota(jnp.int32, (rows, MLA_V), 0) % n_heads
        out = jnp.zeros((rows, MLA_V), f32)
        for h in range(n_heads):
            out = out + jnp.where(head == h, full[:, h * MLA_V:(h + 1) * MLA_V], 0.0)
        o_ref[0] = out.astype(o_ref.dtype)


def _mla_sample(page_table, q_lat, q_rope, w_uk_t, w_uv, rows_new, cache, *, n_heads, s_new):
    n, r, kl = q_lat.shape
    n_pages = page_table.shape[1]
    pps = _pages_per_step(n_pages)
    return pl.pallas_call(
        functools.partial(_mla_sample_kernel, pps=pps, n_heads=n_heads, s_new=s_new, kv_lora=kl),
        grid_spec=pltpu.PrefetchScalarGridSpec(
            num_scalar_prefetch=1,
            grid=(n, n_pages // pps),
            in_specs=[_per_seq(q_lat.shape), _per_seq(q_rope.shape), _shared(w_uk_t.shape),
                      _shared(w_uv.shape), _per_seq(rows_new.shape)]
            + _page_specs(kl + MLA_ROPE, n_pages, pps, False),
            out_specs=_per_seq((n, r, MLA_V)),
            scratch_shapes=[pltpu.VMEM((w_uk_t.shape[0] + r, kl), bf16),
                            pltpu.VMEM((r, 1), f32), pltpu.VMEM((r, 1), f32),
                            pltpu.VMEM((r, kl), f32)]),
        out_shape=jax.ShapeDtypeStruct((n, r, MLA_V), bf16),
        compiler_params=_params("parallel", "arbitrary"),
    )(page_table, q_lat, q_rope, w_uk_t, w_uv, rows_new, *([cache] * pps))


def _fox_sample_kernel(pt_ref, q_ref, fn_ref, fnl_ref, new_ref, *rest, pps, group, s_new):
    kv_pages = rest[:pps]
    lf_pages = rest[pps:2 * pps]
    o_ref, k_s, v_s, m_ref, l_ref, acc_ref, carry_ref = rest[2 * pps:]
    c = pl.program_id(1)
    d = HEAD_DIM
    kvw = FOX_KV_HEADS * d
    n_heads = FOX_KV_HEADS * group
    rows = s_new * group

    @pl.when(c == 0)
    def _():
        _softmax_init(m_ref, l_ref, acc_ref)
        carry_ref[...] = jnp.zeros_like(carry_ref)

    later = jnp.where(lax.broadcasted_iota(jnp.int32, (PAGE_SIZE, PAGE_SIZE), 0)
                      > lax.broadcasted_iota(jnp.int32, (PAGE_SIZE, PAGE_SIZE), 1), 1.0, 0.0).astype(bf16)
    parts = [part for t in range(pps) for part in _split3(lf_pages[t][...])]
    sums = _dot(jnp.concatenate(parts, axis=0), later)
    carry = carry_ref[...]
    after = []
    for t in range(pps):
        suffix = (sums[(3 * t) * n_heads:(3 * t + 1) * n_heads]
                  + sums[(3 * t + 1) * n_heads:(3 * t + 2) * n_heads]
                  + sums[(3 * t + 2) * n_heads:(3 * t + 3) * n_heads])
        after.append(suffix + carry)
        carry = carry + (suffix[:, 0:1] + lf_pages[t][:, 0:1])
    carry_ref[...] = carry
    for kh in range(FOX_KV_HEADS):
        for t in range(pps):
            rows_t = slice(t * PAGE_SIZE, (t + 1) * PAGE_SIZE)
            k_s[kh, rows_t, :] = _kv_rows(kv_pages[t], kh, FOX_KV_HEADS).astype(bf16)
            v_s[kh, rows_t, :] = _kv_rows(kv_pages[t], FOX_KV_HEADS + kh, FOX_KV_HEADS).astype(bf16)
        bias = jnp.concatenate(
            [jnp.concatenate([after[t][kh * group:(kh + 1) * group]] * s_new, axis=0)
             for t in range(pps)], axis=1)
        s = _dot_nt(q_ref[0, kh], k_s[kh]) + (bias + fn_ref[0, kh])
        _softmax_step(s, v_s[kh], m_ref, l_ref, acc_ref, (kh,))

    @pl.when(c == pl.num_programs(1) - 1)
    def _():
        mask = _new_key_mask(rows, group, s_new)
        for kh in range(FOX_KV_HEADS):
            kb = new_ref[0, :, kh * d:(kh + 1) * d].astype(bf16)
            vb = new_ref[0, :, kvw + kh * d:kvw + (kh + 1) * d].astype(bf16)
            s = _dot_nt(q_ref[0, kh], kb) + (fn_ref[0, kh] - fnl_ref[0, kh])
            _softmax_step(jnp.where(mask, s, NEG_INF), vb, m_ref, l_ref, acc_ref, (kh,))
            o_ref[0, kh] = (acc_ref[kh] / l_ref[kh]).astype(o_ref.dtype)


def _fox_sample(page_table, q, fn, fnl, kv_new, cache_kv, cache_logf, *, group, s_new):
    n, kh, r, d = q.shape
    n_pages = page_table.shape[1]
    pps = _pages_per_step(n_pages)
    n_heads = kh * group
    return pl.pallas_call(
        functools.partial(_fox_sample_kernel, pps=pps, group=group, s_new=s_new),
        grid_spec=pltpu.PrefetchScalarGridSpec(
            num_scalar_prefetch=1,
            grid=(n, n_pages // pps),
            in_specs=[_per_seq(q.shape), _per_seq(fn.shape), _per_seq(fnl.shape),
                      _per_seq(kv_new.shape)]
            + _page_specs(PAGE_SIZE * 2 * kh, n_pages, pps, True)
            + _page_specs(n_heads, n_pages, pps, True),
            out_specs=_per_seq(q.shape),
            scratch_shapes=[pltpu.VMEM((kh, pps * PAGE_SIZE, d), bf16),
                            pltpu.VMEM((kh, pps * PAGE_SIZE, d), bf16),
                            pltpu.VMEM((kh, r, 1), f32), pltpu.VMEM((kh, r, 1), f32),
                            pltpu.VMEM((kh, r, d), f32), pltpu.VMEM((n_heads, LANES), f32)]),
        out_shape=jax.ShapeDtypeStruct(q.shape, bf16),
        compiler_params=_params("parallel", "arbitrary"),
    )(page_table, q, fn, fnl, kv_new, *([cache_kv] * pps), *([cache_logf] * pps))


def _t5_bucket(dist):
    n = jnp.maximum(dist, 0)
    exact = RPE_BUCKETS // 2
    nf = jnp.maximum(n, 1).astype(f32)
    log_b = exact + (jnp.log(nf / exact) / math.log(RPE_MAX_DIST / exact)
                     * (RPE_BUCKETS - exact)).astype(jnp.int32)
    return jnp.where(n < exact, n, jnp.minimum(log_b, RPE_BUCKETS - 1))


def _toeplitz(g):
    n = (g.shape[0] + 1) // 2
    u = jnp.concatenate([g[n - 1::-1], jnp.zeros((1, g.shape[1]), g.dtype), g[:n - 1:-1]], axis=0)
    b = jnp.tile(u, (n, 1))[:n * (2 * n - 1)].reshape(n, 2 * n - 1, -1)[:, :n]
    return b.transpose(2, 0, 1)


def _rope_tables(pos):
    inv = ROPE_THETA ** (-jnp.arange(0, MLA_ROPE, 2, dtype=f32) / MLA_ROPE)
    ang = pos.astype(f32)[:, None] * inv[None, :]
    cos, sin = jnp.cos(ang), jnp.sin(ang)
    z = jnp.zeros((pos.shape[0], LANES - MLA_ROPE), f32)
    return jnp.concatenate([cos, cos, z], axis=1), jnp.concatenate([-sin, sin, z], axis=1)


def _pad_cols(w, width):
    return jnp.pad(w, ((0, 0), (0, width - w.shape[1])))


def _pad_rows(a, rows):
    return jnp.pad(a, ((0, 0), (0, rows - a.shape[1]), (0, 0)))


def _to_kv_rows(a, n, s, kvh, group):
    return a.reshape(n, s, kvh, group, HEAD_DIM).transpose(0, 2, 1, 3, 4).reshape(
        n, kvh, s * group, HEAD_DIM)


def _from_kv_rows(a, n, s, kvh, group):
    return a.reshape(n, kvh, s, group, HEAD_DIM).transpose(0, 2, 1, 3, 4).reshape(
        n * s, kvh * group * HEAD_DIM)


def kernel(x_prompt, x_sample, c_prompt, c_sample, cache_moba_kv, cache_mla, cache_fox_kv,
           cache_fox_logf, page_table, rpe_table, norm_mix_g, norm_ffn_g, w_ada, b_ada,
           w_in_even, moba_q_g, moba_k_g, mla_q_a_g, w_uq, mla_kv_a_g, w_ukv, mla_q_g, mla_k_g,
           w_out_even, w_in_odd, fox_f_b, fox_q_g, fox_k_g, w_out_odd, w_gate_up, w_down):
    nb_, t, dm = x_prompt.shape
    n, s_new, _ = x_sample.shape
    n_pages = page_table.shape[1]
    past = n_pages * PAGE_SIZE
    assert t % MOBA_BLOCK == 0 and n_pages % 2 == 0 and MOBA_BLOCK >= RPE_MAX_DIST
    moba_heads = dm // (2 * HEAD_DIM)
    mla_heads = dm // (2 * HEAD_DIM)
    fox_heads = dm // HEAD_DIM
    q_lora, kv_lora = dm // 4, dm // 8
    moba_group = moba_heads // MOBA_KV_HEADS
    fox_group = fox_heads // FOX_KV_HEADS
    n_moba_blocks = past // MOBA_BLOCK
    tm = 512

    w_in_even_p = _pad_cols(w_in_even[0], w_in_even.shape[2] + LANES - MLA_ROPE).astype(bf16)
    w_uq_p = jnp.pad(w_uq[0].reshape(q_lora, mla_heads, MLA_QK),
                     ((0, 0), (0, 0), (0, 2 * LANES - MLA_QK))).reshape(q_lora, -1).astype(bf16)
    w_ukv_b = w_ukv[0].astype(bf16)
    w_ukv_h = w_ukv[0].reshape(kv_lora, mla_heads, MLA_NOPE + MLA_V)
    w_uk_t = w_ukv_h[:, :, :MLA_NOPE].reshape(kv_lora, -1).T.astype(bf16)
    w_uv = w_ukv_h[:, :, MLA_NOPE:].reshape(kv_lora, -1).astype(bf16)
    w_in_odd_p = _pad_cols(w_in_odd[0], w_in_odd.shape[2] + LANES - fox_heads).astype(bf16)
    fox_b_p = _pad_cols(fox_f_b[0][None], LANES)
    qk_gain = (jnp.concatenate([mla_q_g[0], mla_q_g[0][MLA_NOPE:]])
               * jnp.concatenate([mla_k_g[0], mla_k_g[0][MLA_NOPE:]]) * MLA_SCALE)
    qk_gain = _pad_cols(qk_gain[None], 2 * LANES)
    even_gains = (moba_q_g[0][None], moba_k_g[0][None], mla_q_a_g[0][None], mla_kv_a_g[0][None])

    offsets = jnp.arange(-(MOBA_BLOCK - 1), MOBA_BLOCK)
    b_own = _toeplitz(rpe_table[_t5_bucket(offsets)])
    b_prev = _toeplitz(rpe_table[_t5_bucket(offsets + MOBA_BLOCK)])
    far = rpe_table[_t5_bucket(jnp.array(MOBA_BLOCK + 1))]
    b_far = jnp.broadcast_to(far[:, None, None], (moba_heads, 1, MOBA_BLOCK))
    s_idx = jnp.arange(s_new)
    chunk_keys = _pages_per_step(n_pages) * PAGE_SIZE
    assert chunk_keys % MOBA_BLOCK == 0
    d_last = (past + s_idx)[:, None] - (past - chunk_keys + jnp.arange(chunk_keys))[None, :]
    sb_last = rpe_table[_t5_bucket(d_last)].transpose(2, 0, 1)
    d_own = s_idx[:, None] - jnp.arange(LANES)[None, :]
    sb_own = rpe_table[_t5_bucket(d_own)].transpose(2, 0, 1)
    rows_sg = lambda b: b.reshape(MOBA_KV_HEADS, moba_group, s_new, -1).transpose(0, 2, 1, 3).reshape(
        MOBA_KV_HEADS, s_new * moba_group, -1)
    sb_last, sb_own = rows_sg(sb_last), rows_sg(sb_own)
    sb_far = jnp.broadcast_to(rows_sg(jnp.broadcast_to(far[:, None, None], (moba_heads, s_new, 1))),
                              (MOBA_KV_HEADS, s_new * moba_group, chunk_keys))

    cos_p, sin_p = _rope_tables(jnp.arange(t))
    cos_s, sin_s = _rope_tables(past + jnp.arange(s_new))
    cos_s, sin_s = jnp.tile(cos_s, (n, 1)), jnp.tile(sin_s, (n, 1))

    caches = dict(
        moba=cache_moba_kv.reshape(cache_moba_kv.shape[0], -1, PAGE_SIZE * 4, HEAD_DIM),
        mla=cache_mla.transpose(0, 1, 3, 2),
        fox=cache_fox_kv.reshape(cache_fox_kv.shape[0], -1, PAGE_SIZE * 4, HEAD_DIM),
        logf=cache_fox_logf.transpose(0, 1, 3, 2))

    c_all = jnp.concatenate([c_prompt, c_sample], axis=0)

    def modulation(layer):
        mod = _matmul(c_all, w_ada[layer].astype(bf16), out_dtype=f32, tm=c_all.shape[0], tn=1024,
                      bias=b_ada[layer][None], silu_in=True)
        parts = jnp.split(mod, 6, axis=-1)
        prompt = [p[:nb_].reshape(nb_, 1, dm) for p in parts]
        sample = [jnp.repeat(p[nb_:], s_new, axis=0).reshape(1, n * s_new, dm) for p in parts]
        return prompt, sample

    def ffn(x, layer, sc, sh, gate):
        h = _norm_mod(x, norm_ffn_g[layer][None], sc, sh, tm)
        a = _matmul(h, w_gate_up[layer].astype(bf16), out_dtype=bf16, tm=1024, tn=512, swiglu=True)
        return _matmul(a, w_down[layer].astype(bf16), out_dtype=f32, tm=1024, tn=512,
                       res=x, gate=gate)

    def even_front(x, sc, sh, cos_t, sin_t, with_kmean):
        h = _norm_mod(x, norm_mix_g[0][None], sc, sh, tm)
        z = _matmul(h, w_in_even_p, out_dtype=f32, tm=tm, tn=w_in_even_p.shape[1])
        qa, kva, kb, vb, cqn, rows, ckvn, kpe, *kmean = _even_post(
            z, cos_t, sin_t, even_gains, tm=tm, n_heads=moba_heads, q_lora=q_lora, kv_lora=kv_lora,
            with_kmean=with_kmean)
        qb = _mla_q(cqn, w_uq_p, cos_t, sin_t, qk_gain, tm=tm, n_heads=mla_heads)
        return qa, kva, kb, vb, rows, ckvn, kpe, kmean, qb

    def odd_front(x, sc, sh, group_rows):
        h = _norm_mod(x, norm_mix_g[1][None], sc, sh, tm)
        z = _matmul(h, w_in_odd_p, out_dtype=f32, tm=tm, tn=w_in_odd_p.shape[1])
        return _odd_post(z, fox_q_g[0][None], fox_k_g[0][None], fox_b_p, tm=tm, n_heads=fox_heads,
                         group=group_rows)

    (sh1, sc1, g1, sh2, sc2, g2), mods_s0 = modulation(0)
    x = x_prompt.reshape(nb_ * t, dm)
    qa, kva, kb, vb, rows, ckvn, kpe, kmean, qb = even_front(x, sc1, sh1, cos_p, sin_p, True)
    seq = lambda a: a.reshape(nb_, t, -1)
    o_a = _moba_prompt(seq(qa), seq(kb), seq(vb), kmean[0].reshape(nb_, t // MOBA_BLOCK, -1),
                       b_own, b_prev, b_far)
    k_b, v_b = _mla_kv(ckvn, kpe, w_ukv_b, tm=tm, n_heads=mla_heads)
    o_b = _mla_prompt(seq(qb), seq(k_b), seq(v_b), n_heads=mla_heads, tq=512)
    o = jnp.concatenate([o_a, o_b], axis=-1).reshape(nb_ * t, -1)
    x = _matmul(o, w_out_even[0].astype(bf16), out_dtype=f32, tm=1024, tn=512, res=x, gate=g1)
    x = ffn(x, 0, sc2, sh2, g2)
    new_moba_p = kva.reshape(1, nb_, t, 2, MOBA_KV_HEADS, HEAD_DIM)
    new_mla_p = rows.reshape(1, nb_, t, -1)

    (sh1, sc1, g1, sh2, sc2, g2), mods_s1 = modulation(1)
    q, kv, kb, vb, logf, cum = odd_front(x, sc1, sh1, t)
    tq = min(512, t)
    cum_h = cum.reshape(nb_, t, FOX_KV_HEADS, fox_group)
    fq = cum_h.transpose(0, 2, 1, 3)
    fk = cum_h.reshape(nb_, t // tq, tq, FOX_KV_HEADS, fox_group).transpose(0, 3, 1, 4, 2)
    o = _fox_prompt(seq(q), seq(kb), seq(vb), fq, fk, tq=tq).reshape(nb_ * t, -1)
    x = _matmul(o, w_out_odd[0].astype(bf16), out_dtype=f32, tm=1024, tn=512, res=x, gate=g1)
    y_prompt = ffn(x, 1, sc2, sh2, g2).reshape(nb_, t, dm)
    new_fox_p = kv.reshape(1, nb_, t, 2, FOX_KV_HEADS, HEAD_DIM)
    new_logf_p = logf.reshape(1, nb_, t, fox_heads)

    sh1, sc1, g1, sh2, sc2, g2 = mods_s0
    x = x_sample.reshape(n * s_new, dm)
    qa, kva, kb, vb, rows, ckvn, kpe, _, qb = even_front(x, sc1, sh1, cos_s, sin_s, False)
    qa_r = _to_kv_rows(qa, n, s_new, MOBA_KV_HEADS, moba_group)
    kva_pad = _pad_rows(kva.reshape(n, s_new, -1), PAGE_SIZE)
    o_a = _moba_sample(page_table, qa_r, sb_last, sb_far, sb_own, kva_pad, caches["moba"],
                       group=moba_group, s_new=s_new)
    o_a = _from_kv_rows(o_a, n, s_new, MOBA_KV_HEADS, moba_group)
    q_lat = _mla_absorb(qb, w_uk_t, mla_heads).reshape(n, s_new * mla_heads, kv_lora)
    q_rope = qb.reshape(n, s_new * mla_heads, 2 * LANES)[:, :, MLA_NOPE:MLA_QK]
    rows_pad = _pad_rows(rows.reshape(n, s_new, -1), PAGE_SIZE).transpose(0, 2, 1)
    o_b = _mla_sample(page_table, q_lat, q_rope, w_uk_t, w_uv, rows_pad, caches["mla"],
                      n_heads=mla_heads, s_new=s_new).reshape(n * s_new, -1)
    o = jnp.concatenate([o_a, o_b], axis=-1)
    x = _matmul(o, w_out_even[0].astype(bf16), out_dtype=f32, tm=1024, tn=512, res=x, gate=g1)
    x = ffn(x, 0, sc2, sh2, g2)
    new_moba_s = kva.reshape(1, n, s_new, 2, MOBA_KV_HEADS, HEAD_DIM)
    new_mla_s = rows.reshape(1, n, s_new, -1)

    sh1, sc1, g1, sh2, sc2, g2 = mods_s1
    q, kv, kb, vb, logf, cum = odd_front(x, sc1, sh1, s_new)
    q_r = _to_kv_rows(q, n, s_new, FOX_KV_HEADS, fox_group)
    cum_s = cum.reshape(n, s_new, FOX_KV_HEADS, fox_group)
    fn = cum_s.transpose(0, 2, 1, 3).reshape(n, FOX_KV_HEADS, s_new * fox_group, 1)
    fnl = jnp.broadcast_to(cum_s.transpose(0, 2, 3, 1)[:, :, None],
                           (n, FOX_KV_HEADS, s_new, fox_group, s_new)).reshape(
        n, FOX_KV_HEADS, s_new * fox_group, s_new)
    fnl = jnp.pad(fnl, ((0, 0), (0, 0), (0, 0), (0, LANES - s_new)))
    kv_pad = _pad_rows(kv.reshape(n, s_new, -1), PAGE_SIZE)
    o = _fox_sample(page_table, q_r, fn, fnl, kv_pad, caches["fox"], caches["logf"],
                    group=fox_group, s_new=s_new)
    o = _from_kv_rows(o, n, s_new, FOX_KV_HEADS, fox_group)
    x = _matmul(o, w_out_odd[0].astype(bf16), out_dtype=f32, tm=1024, tn=512, res=x, gate=g1)
    y_sample = ffn(x, 1, sc2, sh2, g2).reshape(n, s_new, dm)
    new_fox_s = kv.reshape(1, n, s_new, 2, FOX_KV_HEADS, HEAD_DIM)
    new_logf_s = logf.reshape(1, n, s_new, fox_heads)

    return (y_prompt, y_sample, new_moba_p, new_moba_s, new_mla_p, new_mla_s,
            new_fox_p, new_fox_s, new_logf_p, new_logf_s)
```

```python
import functools
import math

import numpy as np
import jax
import jax.numpy as jnp
from jax import lax
from jax.experimental import pallas as pl
from jax.experimental.pallas import tpu as pltpu

f32 = jnp.float32
bf16 = jnp.bfloat16

HEAD_DIM = 128
MOBA_KV_HEADS = 2
MOBA_BLOCK = 256
MOBA_TOPK = 3
MLA_NOPE = 128
MLA_ROPE = 64
MLA_V = 128
MLA_QK = MLA_NOPE + MLA_ROPE
ROPE_THETA = 10000.0
FOX_KV_HEADS = 2
RPE_BUCKETS = 32
RPE_MAX_DIST = 128
PAGE_SIZE = 128
EPS = 1e-6
NEG_INF = -1e30
ATTN_SCALE = HEAD_DIM ** -0.5
MLA_SCALE = MLA_QK ** -0.5

LANES = 128
VMEM_LIMIT = 56 * 1024 * 1024
PAGES_PER_STEP = 32
MLA_KEY_STEP = 512

_NT = (((1,), (1,)), ((), ()))


def _dot(a, b):
    return jnp.dot(a, b, preferred_element_type=f32)


def _dot_nt(a, b):
    return lax.dot_general(a, b, _NT, preferred_element_type=f32)


def _split3(x):
    hi = x.astype(bf16)
    r1 = x - hi.astype(f32)
    mid = r1.astype(bf16)
    lo = (r1 - mid.astype(f32)).astype(bf16)
    return hi, mid, lo


def _rms(x, g):
    return x * lax.rsqrt(jnp.mean(x * x, axis=-1, keepdims=True) + EPS) * g


def _params(*sem):
    return pltpu.CompilerParams(dimension_semantics=sem, vmem_limit_bytes=VMEM_LIMIT)


def _mm_kernel(*refs, nk, swiglu, has_bias, has_res, silu_in):
    it = iter(refs)
    a_ref = next(it)
    w_ref = next(it)
    w2_ref = next(it) if swiglu else None
    b_ref = next(it) if has_bias else None
    x_ref = next(it) if has_res else None
    g_ref = next(it) if has_res else None
    o_ref = next(it)
    accs = list(it)

    a = a_ref[...]
    if silu_in:
        a = a * jax.nn.sigmoid(a)
    a = a.astype(bf16)

    def finish(acc, acc2):
        r = acc
        if swiglu:
            r = (acc * jax.nn.sigmoid(acc)) * acc2
        if has_bias:
            r = r + b_ref[...]
        if has_res:
            r = x_ref[...] + g_ref[0] * r
        o_ref[...] = r.astype(o_ref.dtype)

    if nk == 1:
        finish(_dot(a, w_ref[...]), _dot(a, w2_ref[...]) if swiglu else None)
        return

    k = pl.program_id(2)

    @pl.when(k == 0)
    def _():
        for acc in accs:
            acc[...] = jnp.zeros_like(acc)

    accs[0][...] += _dot(a, w_ref[...])
    if swiglu:
        accs[1][...] += _dot(a, w2_ref[...])

    @pl.when(k == nk - 1)
    def _():
        finish(accs[0][...], accs[1][...] if swiglu else None)


def _matmul(a, w, *, out_dtype, tm, tn, tk=None, swiglu=False, bias=None, res=None,
            gate=None, silu_in=False):
    m, kd = a.shape
    n = w.shape[1] // 2 if swiglu else w.shape[1]
    tm = min(tm, m)
    tn = min(tn, n)
    tk = kd if tk is None else tk
    assert m % tm == 0 and n % tn == 0 and kd % tk == 0
    nk = kd // tk
    nj = n // tn
    in_specs = [pl.BlockSpec((tm, tk), lambda i, j, k: (i, k)),
                pl.BlockSpec((tk, tn), lambda i, j, k: (k, j))]
    args = [a, w]
    if swiglu:
        in_specs.append(pl.BlockSpec((tk, tn), lambda i, j, k: (k, j + nj)))
        args.append(w)
    if bias is not None:
        in_specs.append(pl.BlockSpec((1, tn), lambda i, j, k: (0, j)))
        args.append(bias)
    if res is not None:
        tiles_per_group = (m // tm) // gate.shape[0]
        in_specs.append(pl.BlockSpec((tm, tn), lambda i, j, k: (i, j)))
        in_specs.append(pl.BlockSpec((1, gate.shape[1], tn),
                                     lambda i, j, k: (i // tiles_per_group, 0, j)))
        args += [res, gate]
    scratch = []
    if nk > 1:
        scratch = [pltpu.VMEM((tm, tn), f32)] * (2 if swiglu else 1)
    return pl.pallas_call(
        functools.partial(_mm_kernel, nk=nk, swiglu=swiglu, has_bias=bias is not None,
                          has_res=res is not None, silu_in=silu_in),
        grid=(m // tm, nj, nk),
        in_specs=in_specs,
        out_specs=pl.BlockSpec((tm, tn), lambda i, j, k: (i, j)),
        out_shape=jax.ShapeDtypeStruct((m, n), out_dtype),
        scratch_shapes=scratch,
        compiler_params=_params("parallel", "parallel", "arbitrary"),
    )(*args)


def _norm_mod_kernel(x_ref, g_ref, sc_ref, sh_ref, o_ref):
    y = _rms(x_ref[...], g_ref[...])
    o_ref[...] = (y * (1.0 + sc_ref[0]) + sh_ref[0]).astype(o_ref.dtype)


def _norm_mod(x, g, sc, sh, tm):
    m, d = x.shape
    tm = min(tm, m)
    tiles_per_group = (m // tm) // sc.shape[0]
    mod_spec = pl.BlockSpec((1, sc.shape[1], d), lambda i: (i // tiles_per_group, 0, 0))
    return pl.pallas_call(
        _norm_mod_kernel,
        grid=(m // tm,),
        in_specs=[pl.BlockSpec((tm, d), lambda i: (i, 0)),
                  pl.BlockSpec((1, d), lambda i: (0, 0)), mod_spec, mod_spec],
        out_specs=pl.BlockSpec((tm, d), lambda i: (i, 0)),
        out_shape=jax.ShapeDtypeStruct((m, d), bf16),
        compiler_params=_params("parallel"),
    )(x, g, sc, sh)


def _rope128(x, cos_ref, sin_ref):
    lane = lax.broadcasted_iota(jnp.int32, x.shape, 1)
    swapped = jnp.where(lane < MLA_ROPE // 2, pltpu.roll(x, LANES - MLA_ROPE // 2, 1),
                        pltpu.roll(x, MLA_ROPE // 2, 1))
    return x * cos_ref[...] + swapped * sin_ref[...]


def _store_with_ones(vb_ref, v, n_heads):
    d = HEAD_DIM
    for h in range(n_heads):
        vb_ref[:, 2 * h * d:(2 * h + 1) * d] = v[:, h * d:(h + 1) * d].astype(bf16)
        vb_ref[:, (2 * h + 1) * d:(2 * h + 2) * d] = jnp.ones((v.shape[0], d), bf16)


def _even_post_kernel(z_ref, cos_ref, sin_ref, qg_ref, kg_ref, cqg_ref, ckvg_ref,
                      qa_ref, kva_ref, kb_ref, vb_ref, cqn_ref, rows_ref, ckvn_ref, kpe_ref,
                      *maybe_kmean_ref, n_heads, q_lora, kv_lora):
    tm = z_ref.shape[0]
    d = HEAD_DIM
    for h in range(n_heads):
        qa_ref[:, h * d:(h + 1) * d] = _rms(z_ref[:, h * d:(h + 1) * d], qg_ref[...])
    off = n_heads * d
    for h in range(MOBA_KV_HEADS):
        k = _rms(z_ref[:, off + h * d:off + (h + 1) * d], kg_ref[...])
        kva_ref[:, h * d:(h + 1) * d] = k
        kb_ref[:, h * d:(h + 1) * d] = k.astype(bf16)
        for blk in range(tm // MOBA_BLOCK if maybe_kmean_ref else 0):
            maybe_kmean_ref[0][blk, :, h * d:(h + 1) * d] = jnp.mean(
                k[blk * MOBA_BLOCK:(blk + 1) * MOBA_BLOCK], axis=0, keepdims=True)
    off += MOBA_KV_HEADS * d
    v = z_ref[:, off:off + MOBA_KV_HEADS * d]
    kva_ref[:, MOBA_KV_HEADS * d:] = v
    _store_with_ones(vb_ref, v, MOBA_KV_HEADS)
    off += MOBA_KV_HEADS * d
    cqn_ref[...] = _rms(z_ref[:, off:off + q_lora], cqg_ref[...]).astype(bf16)
    off += q_lora
    ckvn = _rms(z_ref[:, off:off + kv_lora], ckvg_ref[...])
    rows_ref[:, :kv_lora] = ckvn
    ckvn_ref[...] = ckvn.astype(bf16)
    off += kv_lora
    kr = _rope128(z_ref[:, off:off + LANES], cos_ref, sin_ref)
    kpe_ref[...] = kr
    rows_ref[:, kv_lora:] = kr[:, :MLA_ROPE]


def _even_post(z, cos_t, sin_t, prm_g, *, tm, n_heads, q_lora, kv_lora, with_kmean):
    m = z.shape[0]
    tm = min(tm, m)
    t_tiles = cos_t.shape[0] // tm
    d = HEAD_DIM
    row = lambda w: pl.BlockSpec((tm, w), lambda i: (i, 0))
    vec = lambda w: pl.BlockSpec((1, w), lambda i: (0, 0))
    tab = pl.BlockSpec((tm, LANES), lambda i: (i % t_tiles, 0))
    nblk = tm // MOBA_BLOCK
    kvw = MOBA_KV_HEADS * d
    outs = [
        jax.ShapeDtypeStruct((m, n_heads * d), f32),
        jax.ShapeDtypeStruct((m, 2 * kvw), f32),
        jax.ShapeDtypeStruct((m, kvw), bf16),
        jax.ShapeDtypeStruct((m, 2 * kvw), bf16),
        jax.ShapeDtypeStruct((m, q_lora), bf16),
        jax.ShapeDtypeStruct((m, kv_lora + MLA_ROPE), f32),
        jax.ShapeDtypeStruct((m, kv_lora), bf16),
        jax.ShapeDtypeStruct((m, LANES), f32),
    ]
    out_specs = [row(n_heads * d), row(2 * kvw), row(kvw), row(2 * kvw), row(q_lora),
                 row(kv_lora + MLA_ROPE), row(kv_lora), row(LANES)]
    if with_kmean:
        assert tm % MOBA_BLOCK == 0
        outs.append(jax.ShapeDtypeStruct((m // MOBA_BLOCK, 1, kvw), f32))
        out_specs.append(pl.BlockSpec((nblk, 1, kvw), lambda i: (i, 0, 0)))
    return pl.pallas_call(
        functools.partial(_even_post_kernel, n_heads=n_heads, q_lora=q_lora, kv_lora=kv_lora),
        grid=(m // tm,),
        in_specs=[row(z.shape[1]), tab, tab, vec(d), vec(d), vec(q_lora), vec(kv_lora)],
        out_specs=out_specs,
        out_shape=outs,
        compiler_params=_params("parallel"),
    )(z, cos_t, sin_t, *prm_g)


def _mla_q_kernel(c_ref, w_ref, cos_ref, sin_ref, g_ref, o_ref):
    acc = _dot(c_ref[...], w_ref[...])
    nope = acc[:, :MLA_NOPE]
    rope = _rope128(acc[:, MLA_NOPE:], cos_ref, sin_ref)
    ss = jnp.sum(nope * nope, axis=-1, keepdims=True) + jnp.sum(rope * rope, axis=-1, keepdims=True)
    inv = lax.rsqrt(ss * (1.0 / MLA_QK) + EPS)
    o_ref[:, :MLA_NOPE] = (nope * inv * g_ref[:, :MLA_NOPE]).astype(o_ref.dtype)
    o_ref[:, MLA_NOPE:] = (rope * inv * g_ref[:, MLA_NOPE:]).astype(o_ref.dtype)


def _mla_q(cqn, w_uq_pad, cos_t, sin_t, gain, *, tm, n_heads):
    m, ql = cqn.shape
    tm = min(tm, m)
    t_tiles = cos_t.shape[0] // tm
    tab = pl.BlockSpec((tm, LANES), lambda i, h: (i % t_tiles, 0))
    return pl.pallas_call(
        _mla_q_kernel,
        grid=(m // tm, n_heads),
        in_specs=[pl.BlockSpec((tm, ql), lambda i, h: (i, 0)),
                  pl.BlockSpec((ql, 2 * LANES), lambda i, h: (0, h)), tab, tab,
                  pl.BlockSpec((1, 2 * LANES), lambda i, h: (0, 0))],
        out_specs=pl.BlockSpec((tm, 2 * LANES), lambda i, h: (i, h)),
        out_shape=jax.ShapeDtypeStruct((m, n_heads * 2 * LANES), bf16),
        compiler_params=_params("parallel", "arbitrary"),
    )(cqn, w_uq_pad, cos_t, sin_t, gain)


def _mla_kv_kernel(c_ref, kpe_ref, w_ref, k_ref, v_ref):
    acc = _dot(c_ref[...], w_ref[...])
    kn = acc[:, :MLA_NOPE]
    kp = kpe_ref[...]
    ss = jnp.sum(kn * kn, axis=-1, keepdims=True) + jnp.sum(kp * kp, axis=-1, keepdims=True)
    inv = lax.rsqrt(ss * (1.0 / MLA_QK) + EPS)
    k_ref[:, :MLA_NOPE] = (kn * inv).astype(bf16)
    k_ref[:, MLA_NOPE:] = (kp * inv).astype(bf16)
    v_ref[:, :MLA_V] = acc[:, MLA_NOPE:].astype(bf16)
    v_ref[:, MLA_V:] = jnp.ones((acc.shape[0], LANES), bf16)


def _mla_kv(ckvn, kpe, w_ukv, *, tm, n_heads):
    m, kl = ckvn.shape
    tm = min(tm, m)
    return pl.pallas_call(
        _mla_kv_kernel,
        grid=(m // tm, n_heads),
        in_specs=[pl.BlockSpec((tm, kl), lambda i, h: (i, 0)),
                  pl.BlockSpec((tm, LANES), lambda i, h: (i, 0)),
                  pl.BlockSpec((kl, 2 * LANES), lambda i, h: (0, h))],
        out_specs=[pl.BlockSpec((tm, 2 * LANES), lambda i, h: (i, h)),
                   pl.BlockSpec((tm, MLA_V + LANES), lambda i, h: (i, h))],
        out_shape=[jax.ShapeDtypeStruct((m, n_heads * 2 * LANES), bf16),
                   jax.ShapeDtypeStruct((m, n_heads * (MLA_V + LANES)), bf16)],
        compiler_params=_params("parallel", "arbitrary"),
    )(ckvn, kpe, w_ukv)


def _odd_post_kernel(z_ref, qg_ref, kg_ref, fb_ref, q_ref, kv_ref, kb_ref, vb_ref,
                     logf_ref, cum_ref, carry_ref, *, n_heads, group, tiles_per_seq):
    tm = z_ref.shape[0]
    d = HEAD_DIM
    for h in range(n_heads):
        q_ref[:, h * d:(h + 1) * d] = (_rms(z_ref[:, h * d:(h + 1) * d], qg_ref[...])
                                       * ATTN_SCALE).astype(q_ref.dtype)
    off = n_heads * d
    for h in range(FOX_KV_HEADS):
        k = _rms(z_ref[:, off + h * d:off + (h + 1) * d], kg_ref[...])
        kv_ref[:, h * d:(h + 1) * d] = k
        kb_ref[:, h * d:(h + 1) * d] = k.astype(bf16)
    off += FOX_KV_HEADS * d
    v = z_ref[:, off:off + FOX_KV_HEADS * d]
    kv_ref[:, FOX_KV_HEADS * d:] = v
    _store_with_ones(vb_ref, v, FOX_KV_HEADS)
    off += FOX_KV_HEADS * d
    x = z_ref[:, off:off + LANES] + fb_ref[...]
    logf = jnp.minimum(x, 0.0) - jnp.log1p(jnp.exp(-jnp.abs(x)))
    logf_ref[...] = logf[:, :n_heads]

    r = lax.broadcasted_iota(jnp.int32, (tm, tm), 0)
    c = lax.broadcasted_iota(jnp.int32, (tm, tm), 1)
    tri = (c <= r) if group >= tm else ((c <= r) & (c // group == r // group))
    tri = jnp.where(tri, 1.0, 0.0).astype(bf16)
    hi, mid, lo = _split3(logf)
    cum = _dot(tri, hi) + _dot(tri, mid) + _dot(tri, lo)
    if group >= tm:
        first = pl.program_id(0) % tiles_per_seq == 0

        @pl.when(first)
        def _():
            carry_ref[...] = jnp.zeros_like(carry_ref)

        cum = cum + carry_ref[...]
        carry_ref[...] = cum[tm - 1:tm, :]
    cum_ref[...] = cum[:, :n_heads]


def _odd_post(z, qg, kg, fb_pad, *, tm, n_heads, group):
    m = z.shape[0]
    tm = min(tm, m)
    assert group >= tm and group % tm == 0 or tm % group == 0
    d = HEAD_DIM
    kvw = FOX_KV_HEADS * d
    row = lambda w: pl.BlockSpec((tm, w), lambda i: (i, 0))
    vec = lambda w: pl.BlockSpec((1, w), lambda i: (0, 0))
    outs = [jax.ShapeDtypeStruct((m, n_heads * d), bf16),
            jax.ShapeDtypeStruct((m, 2 * kvw), f32),
            jax.ShapeDtypeStruct((m, kvw), bf16),
            jax.ShapeDtypeStruct((m, 2 * kvw), bf16),
            jax.ShapeDtypeStruct((m, n_heads), f32),
            jax.ShapeDtypeStruct((m, n_heads), f32)]
    return pl.pallas_call(
        functools.partial(_odd_post_kernel, n_heads=n_heads, group=group,
                          tiles_per_seq=max(group // tm, 1)),
        grid=(m // tm,),
        in_specs=[row(z.shape[1]), vec(d), vec(d), vec(LANES)],
        out_specs=[row(n_heads * d), row(2 * kvw), row(kvw), row(2 * kvw), row(n_heads), row(n_heads)],
        out_shape=outs,
        scratch_shapes=[pltpu.VMEM((1, LANES), f32)],
        compiler_params=_params("arbitrary"),
    )(z, qg, kg, fb_pad)


def _softmax_init(m_ref, l_ref, acc_ref):
    m_ref[...] = jnp.full_like(m_ref, NEG_INF)
    if l_ref is not None:
        l_ref[...] = jnp.zeros_like(l_ref)
    acc_ref[...] = jnp.zeros_like(acc_ref)


def _normalised(acc, dv):
    return acc[:, :dv] / acc[:, dv:dv + LANES][:, :dv]


def _softmax_step(s, v, m_ref, l_ref, acc_ref, idx=(), nt=False):
    idx = idx if idx else Ellipsis
    m_prev = m_ref[idx]
    m_new = jnp.maximum(m_prev, jnp.max(s, axis=-1, keepdims=True))
    alpha = jnp.exp(m_prev - m_new)
    p = jnp.exp(s - m_new)
    if l_ref is not None:
        l_ref[idx] = alpha * l_ref[idx] + jnp.sum(p, axis=-1, keepdims=True)
    pb = p.astype(bf16)
    if isinstance(v, (list, tuple)):
        w = s.shape[1] // len(v)
        pv = _dot(pb[:, :w], v[0])
        for t in range(1, len(v)):
            pv = pv + _dot(pb[:, t * w:(t + 1) * w], v[t])
    else:
        pv = _dot_nt(pb, v) if nt else _dot(pb, v)
    acc_ref[idx] = alpha * acc_ref[idx] + pv
    m_ref[idx] = m_new


def _with_ones(v):
    return jnp.concatenate([v, jnp.ones((v.shape[0], LANES), v.dtype)], axis=1)


def _causal(tq, tk, q0, k0):
    r = lax.broadcasted_iota(jnp.int32, (tq, tk), 0) + q0
    c = lax.broadcasted_iota(jnp.int32, (tq, tk), 1) + k0
    return r >= c


def _mla_prompt_kernel(q_ref, k_ref, v_ref, o_ref, m_ref, acc_ref, *, tq):
    i = pl.program_id(2)
    q = q_ref[0]
    _softmax_init(m_ref, None, acc_ref)

    def body(j, carry):
        ks = pl.ds(pl.multiple_of(j * tq, tq), tq)
        _softmax_step(_dot_nt(q, k_ref[0, ks, :]), v_ref[0, ks, :], m_ref, None, acc_ref)
        return carry

    lax.fori_loop(0, i, body, 0)
    ks = pl.ds(pl.multiple_of(i * tq, tq), tq)
    s = jnp.where(_causal(tq, tq, 0, 0), _dot_nt(q, k_ref[0, ks, :]), NEG_INF)
    _softmax_step(s, v_ref[0, ks, :], m_ref, None, acc_ref)
    o_ref[0] = _normalised(acc_ref[...], MLA_V).astype(o_ref.dtype)


def _mla_prompt(q, k, v, *, n_heads, tq):
    b, t, _ = q.shape
    tq = min(tq, t)
    return pl.pallas_call(
        functools.partial(_mla_prompt_kernel, tq=tq),
        grid=(b, n_heads, t // tq),
        in_specs=[pl.BlockSpec((1, tq, 2 * LANES), lambda b_, h, i: (b_, i, h)),
                  pl.BlockSpec((1, t, 2 * LANES), lambda b_, h, i: (b_, 0, h)),
                  pl.BlockSpec((1, t, MLA_V + LANES), lambda b_, h, i: (b_, 0, h))],
        out_specs=pl.BlockSpec((1, tq, MLA_V), lambda b_, h, i: (b_, i, h)),
        out_shape=jax.ShapeDtypeStruct((b, t, n_heads * MLA_V), bf16),
        scratch_shapes=[pltpu.VMEM((tq, 1), f32), pltpu.VMEM((tq, MLA_V + LANES), f32)],
        compiler_params=_params("parallel", "parallel", "arbitrary"),
    )(q, k, v)


def _fox_prompt_kernel(q_ref, k_ref, v_ref, fq_ref, fk_ref, o_ref, m_ref, acc_ref,
                       *, tq, group):
    i = pl.program_id(2)
    d = HEAD_DIM
    _softmax_init(m_ref, None, acc_ref)

    def block(j, diag):
        ks = pl.ds(pl.multiple_of(j * tq, tq), tq)
        kb = k_ref[0, ks, :]
        vb = v_ref[0, ks, :]
        for g in range(group):
            s = _dot_nt(q_ref[0, :, g * d:(g + 1) * d], kb)
            s = s + (fq_ref[0, 0, :, g:g + 1] - fk_ref[0, 0, j, g:g + 1, :])
            if diag:
                s = jnp.where(_causal(tq, tq, 0, 0), s, NEG_INF)
            _softmax_step(s, vb, m_ref, None, acc_ref, (g,))

    def body(j, carry):
        block(j, False)
        return carry

    lax.fori_loop(0, i, body, 0)
    block(i, True)
    for g in range(group):
        o_ref[0, :, g * d:(g + 1) * d] = _normalised(acc_ref[g], d).astype(o_ref.dtype)


def _fox_prompt(q, kb, vb, fq, fk, *, tq):
    b, t, hd = q.shape
    group = hd // HEAD_DIM // FOX_KV_HEADS
    assert fk.shape[-1] == tq
    gd = group * HEAD_DIM
    return pl.pallas_call(
        functools.partial(_fox_prompt_kernel, tq=tq, group=group),
        grid=(b, FOX_KV_HEADS, t // tq),
        in_specs=[pl.BlockSpec((1, tq, gd), lambda b_, h, i: (b_, i, h)),
                  pl.BlockSpec((1, t, HEAD_DIM), lambda b_, h, i: (b_, 0, h)),
                  pl.BlockSpec((1, t, 2 * HEAD_DIM), lambda b_, h, i: (b_, 0, h)),
                  pl.BlockSpec((1, 1, tq, group), lambda b_, h, i: (b_, h, i, 0)),
                  pl.BlockSpec((1, 1, t // tq, group, tq), lambda b_, h, i: (b_, h, 0, 0, 0))],
        out_specs=pl.BlockSpec((1, tq, gd), lambda b_, h, i: (b_, i, h)),
        out_shape=jax.ShapeDtypeStruct((b, t, hd), bf16),
        scratch_shapes=[pltpu.VMEM((group, tq, 1), f32),
                        pltpu.VMEM((group, tq, 2 * HEAD_DIM), f32)],
        compiler_params=_params("parallel", "parallel", "arbitrary"),
    )(q, kb, vb, fq, fk)


def _topk_mask(score, n_valid):
    nb = score.shape[1]
    blk = lax.broadcasted_iota(jnp.int32, score.shape, 1)
    score = jnp.where(blk < n_valid, score, NEG_INF)
    rank = jnp.zeros(score.shape, f32)
    for c in range(nb):
        col = score[:, c:c + 1]
        ahead = jnp.where(col > score, 1.0, jnp.where((col == score) & (blk > c), 1.0, 0.0))
        rank = rank + ahead
    return jnp.where((rank < MOBA_TOPK) & (blk < n_valid), 1.0, 0.0)


def _block_scores(q, km):
    qh, qm, ql = _split3(q)
    kh, kmid, kl = _split3(km)
    return (_dot_nt(qh, kh) + (_dot_nt(qh, kmid) + _dot_nt(qm, kh))
            + (_dot_nt(qh, kl) + _dot_nt(qm, kmid) + _dot_nt(ql, kh)))


def _moba_prompt_kernel(q_ref, k_ref, v_ref, km_ref, bown_ref, bprev_ref, bfar_ref, o_ref,
                        sel_ref, qs_ref, m_ref, acc_ref, *, group):
    i = pl.program_id(2)
    d = HEAD_DIM
    tq = MOBA_BLOCK
    nb = km_ref.shape[1]
    _softmax_init(m_ref, None, acc_ref)
    km = km_ref[0]
    for g in range(group):
        q = q_ref[0, :, g * d:(g + 1) * d]
        sel_ref[g] = _topk_mask(_block_scores(q, km), i)
        qs_ref[g] = (q * ATTN_SCALE).astype(bf16)

    def block(j, kind):
        ks = pl.ds(pl.multiple_of(j * tq, tq), tq)
        kb = k_ref[0, ks, :]
        vb = v_ref[0, ks, :]
        for g in range(group):
            s = _dot_nt(qs_ref[g], kb)
            if kind == "own":
                s = jnp.where(_causal(tq, tq, 0, 0), s + bown_ref[g], NEG_INF)
            else:
                s = s + (bprev_ref[g] if kind == "prev" else bfar_ref[g])
                lane = lax.broadcasted_iota(jnp.int32, (tq, nb), 1)
                picked = jnp.sum(jnp.where(lane == j, sel_ref[g], 0.0), axis=-1, keepdims=True)
                s = jnp.where(picked > 0.5, s, NEG_INF)
            _softmax_step(s, vb, m_ref, None, acc_ref, (g,))

    block(i, "own")

    @pl.when(i >= 1)
    def _():
        block(i - 1, "prev")

    def body(j, carry):
        block(j, "far")
        return carry

    lax.fori_loop(0, i - 1, body, 0)
    for g in range(group):
        o_ref[0, :, g * d:(g + 1) * d] = _normalised(acc_ref[g], d).astype(o_ref.dtype)


def _moba_prompt(qa, kb, vb, kmean, b_own, b_prev, b_far):
    b, t, hd = qa.shape
    group = hd // HEAD_DIM // MOBA_KV_HEADS
    nb = t // MOBA_BLOCK
    gd = group * HEAD_DIM
    blk = MOBA_BLOCK
    bias_spec = pl.BlockSpec((group, blk, blk), lambda b_, h, i: (h, 0, 0))
    return pl.pallas_call(
        functools.partial(_moba_prompt_kernel, group=group),
        grid=(b, MOBA_KV_HEADS, nb),
        in_specs=[pl.BlockSpec((1, blk, gd), lambda b_, h, i: (b_, i, h)),
                  pl.BlockSpec((1, t, HEAD_DIM), lambda b_, h, i: (b_, 0, h)),
                  pl.BlockSpec((1, t, 2 * HEAD_DIM), lambda b_, h, i: (b_, 0, h)),
                  pl.BlockSpec((1, nb, HEAD_DIM), lambda b_, h, i: (b_, 0, h)),
                  bias_spec, bias_spec,
                  pl.BlockSpec((group, 1, blk), lambda b_, h, i: (h, 0, 0))],
        out_specs=pl.BlockSpec((1, blk, gd), lambda b_, h, i: (b_, i, h)),
        out_shape=jax.ShapeDtypeStruct((b, t, hd), bf16),
        scratch_shapes=[pltpu.VMEM((group, blk, nb), f32),
                        pltpu.VMEM((group, blk, HEAD_DIM), bf16),
                        pltpu.VMEM((group, blk, 1), f32),
                        pltpu.VMEM((group, blk, 2 * HEAD_DIM), f32)],
        compiler_params=_params("parallel", "parallel", "arbitrary"),
    )(qa, kb, vb, kmean, b_own, b_prev, b_far)


def _pages_per_step(n_pages):
    return PAGES_PER_STEP if n_pages % PAGES_PER_STEP == 0 else n_pages


def _page_specs(rows, n_pages, pps, reverse):
    def spec(t):
        def index(n, c, pt):
            p = c * pps + t
            return (0, pt[n, n_pages - 1 - p if reverse else p], 0, 0)
        return pl.BlockSpec((None, None, rows, LANES), index)
    return [spec(t) for t in range(pps)]


def _kv_rows(page_ref, which, n_kv_heads):
    return page_ref[pl.ds(which, PAGE_SIZE, stride=2 * n_kv_heads), :]


def _per_seq(shape):
    nd = len(shape)
    return pl.BlockSpec((1,) + tuple(shape[1:]), lambda n, c, pt: (n,) + (0,) * (nd - 1))


def _shared(shape):
    nd = len(shape)
    return pl.BlockSpec(tuple(shape), lambda n, c, pt: (0,) * nd)


def _new_key_mask(rows, group, s_new):
    r = lax.broadcasted_iota(jnp.int32, (rows, LANES), 0) // group
    j = lax.broadcasted_iota(jnp.int32, (rows, LANES), 1)
    return (j <= r) & (j < s_new)


def _moba_sample_kernel(pt_ref, q_ref, blast_ref, bfar_ref, bown_ref, new_ref, *rest,
                        pps, group, s_new):
    pages = rest[:pps]
    o_ref, k_s, v_s, km_s, sel_s, m_ref, l_ref, acc_ref = rest[pps:]
    c = pl.program_id(1)
    nc = pl.num_programs(1)
    d = HEAD_DIM
    kvh = MOBA_KV_HEADS
    rows, nb = sel_s.shape[1], sel_s.shape[2]
    chunk_keys = pps * PAGE_SIZE
    blocks_per_chunk = chunk_keys // MOBA_BLOCK

    for kh in range(kvh):
        sums = []
        for t in range(pps):
            k = _kv_rows(pages[t], kh, kvh)
            v = _kv_rows(pages[t], kvh + kh, kvh)
            r0 = pl.multiple_of((c * pps + t) * PAGE_SIZE, PAGE_SIZE)
            k_s[kh, pl.ds(r0, PAGE_SIZE), :] = k.astype(bf16)
            v_s[kh, pl.ds(r0, PAGE_SIZE), :] = v.astype(bf16)
            sums.append(jnp.sum(k, axis=0, keepdims=True))
        for t in range(0, pps, 2):
            km_s[kh, pl.ds(c * blocks_per_chunk + t // 2, 1), :] = (
                (sums[t] + sums[t + 1]) * (1.0 / MOBA_BLOCK))

    @pl.when(c == nc - 1)
    def _():
        _softmax_init(m_ref, l_ref, acc_ref)
        lane = lax.broadcasted_iota(jnp.int32, (rows, nb), 1)
        for kh in range(kvh):
            sel_s[kh] = _topk_mask(_block_scores(q_ref[0, kh], km_s[kh]), nb)
        qs = [(q_ref[0, kh] * ATTN_SCALE).astype(bf16) for kh in range(kvh)]

        def chunk(ci, carry):
            ks = pl.ds(pl.multiple_of(ci * chunk_keys, chunk_keys), chunk_keys)
            for kh in range(kvh):
                s = _dot_nt(qs[kh], k_s[kh, ks, :])
                bias = jnp.where(ci == nc - 1, blast_ref[kh], bfar_ref[kh])
                sel = sel_s[kh]
                picked = jnp.concatenate(
                    [jnp.broadcast_to(
                        jnp.sum(jnp.where(lane == ci * blocks_per_chunk + b, sel, 0.0),
                                axis=-1, keepdims=True), (rows, MOBA_BLOCK))
                     for b in range(blocks_per_chunk)], axis=1)
                s = jnp.where(picked > 0.5, s + bias, NEG_INF)
                _softmax_step(s, v_s[kh, ks, :], m_ref, l_ref, acc_ref, (kh,))
            return carry

        lax.fori_loop(0, nc, chunk, 0)
        mask = _new_key_mask(rows, group, s_new)
        for kh in range(kvh):
            kb = new_ref[0, :, kh * d:(kh + 1) * d].astype(bf16)
            vb = new_ref[0, :, (kvh + kh) * d:(kvh + kh + 1) * d].astype(bf16)
            s = jnp.where(mask, _dot_nt(qs[kh], kb) + bown_ref[kh], NEG_INF)
            _softmax_step(s, vb, m_ref, l_ref, acc_ref, (kh,))
            o_ref[0, kh] = (acc_ref[kh] / l_ref[kh]).astype(o_ref.dtype)


def _moba_sample(page_table, q, b_last, b_far, b_own, kv_new, cache, *, group, s_new):
    n, kh, r, d = q.shape
    n_pages = page_table.shape[1]
    pps = _pages_per_step(n_pages)
    past = n_pages * PAGE_SIZE
    nb = past // MOBA_BLOCK
    assert pps % 2 == 0
    return pl.pallas_call(
        functools.partial(_moba_sample_kernel, pps=pps, group=group, s_new=s_new),
        grid_spec=pltpu.PrefetchScalarGridSpec(
            num_scalar_prefetch=1,
            grid=(n, n_pages // pps),
            in_specs=[_per_seq(q.shape), _shared(b_last.shape), _shared(b_far.shape),
                      _shared(b_own.shape), _per_seq(kv_new.shape)]
            + _page_specs(PAGE_SIZE * 2 * kh, n_pages, pps, False),
            out_specs=_per_seq(q.shape),
            scratch_shapes=[pltpu.VMEM((kh, past, d), bf16), pltpu.VMEM((kh, past, d), bf16),
                            pltpu.VMEM((kh, nb, d), f32), pltpu.VMEM((kh, r, nb), f32),
                            pltpu.VMEM((kh, r, 1), f32), pltpu.VMEM((kh, r, 1), f32),
                            pltpu.VMEM((kh, r, d), f32)]),
        out_shape=jax.ShapeDtypeStruct(q.shape, bf16),
        compiler_params=_params("arbitrary", "arbitrary"),
    )(page_table, q, b_last, b_far, b_own, kv_new, *([cache] * pps))


def _mla_absorb_kernel(q_ref, w_ref, o_ref):
    o_ref[...] = _dot(q_ref[...], w_ref[...]).astype(o_ref.dtype)


def _mla_absorb(qb, w_uk_t, n_heads):
    m = qb.shape[0]
    kl = w_uk_t.shape[1]
    return pl.pallas_call(
        _mla_absorb_kernel,
        grid=(n_heads,),
        in_specs=[pl.BlockSpec((m, MLA_NOPE), lambda h: (0, 2 * h)),
                  pl.BlockSpec((MLA_NOPE, kl), lambda h: (h, 0))],
        out_specs=pl.BlockSpec((m, kl), lambda h: (0, h)),
        out_shape=jax.ShapeDtypeStruct((m, n_heads * kl), bf16),
        compiler_params=_params("parallel"),
    )(qb, w_uk_t)


def _mla_sample_kernel(pt_ref, ql_ref, qr_ref, wk_ref, wv_ref, new_ref, *rest,
                       pps, n_heads, s_new, kv_lora):
    pages = rest[:pps]
    o_ref, lhs_s, m_ref, l_ref, acc_ref = rest[pps:]
    c = pl.program_id(1)
    rows = s_new * n_heads
    n_up = n_heads * MLA_NOPE

    @pl.when(c == 0)
    def _():
        _softmax_init(m_ref, l_ref, acc_ref)
        lhs_s[:n_up, :] = wk_ref[...]
        lhs_s[n_up:, :] = ql_ref[0]

    def attend(lat_t, kp_t, mask):
        n_keys = lat_t.shape[1]
        step = min(n_keys, MLA_KEY_STEP)
        ssq, s_lat = [], []
        for k0 in range(0, n_keys, step):
            both = _dot(lhs_s[...], lat_t[:, k0:k0 + step])
            ssq.append(jnp.concatenate(
                [jnp.sum(jnp.square(both[h * MLA_NOPE:(h + 1) * MLA_NOPE]), axis=0, keepdims=True)
                 for h in range(n_heads)], axis=0))
            s_lat.append(both[n_up:])
        ssq = jnp.concatenate(ssq, axis=1) + jnp.sum(kp_t * kp_t, axis=0, keepdims=True)
        inv = lax.rsqrt(ssq * (1.0 / MLA_QK) + EPS)
        s = jnp.concatenate(s_lat, axis=1) + _dot(qr_ref[0], kp_t.astype(bf16))
        s = s * jnp.concatenate([inv] * s_new, axis=0)
        if mask is not None:
            s = jnp.where(mask, s, NEG_INF)
        _softmax_step(s, lat_t, m_ref, l_ref, acc_ref, nt=True)

    attend(jnp.concatenate([p[:kv_lora, :].astype(bf16) for p in pages], axis=1),
           jnp.concatenate([p[kv_lora:, :] for p in pages], axis=1), None)

    @pl.when(c == pl.num_programs(1) - 1)
    def _():
        attend(new_ref[0, :kv_lora, :].astype(bf16), new_ref[0, kv_lora:, :],
               _new_key_mask(rows, n_heads, s_new))
        lat = (acc_ref[...] / l_ref[...]).astype(bf16)
        full = _dot(lat, wv_ref[...])
        head = lax.broadcasted_iota(jnp.int32, (rows, MLA_V), 0) % n_heads
        out = jnp.zeros((rows, MLA_V), f32)
        for h in range(n_heads):
            out = out + jnp.where(head == h, full[:, h * MLA_V:(h + 1) * MLA_V], 0.0)
        o_ref[0] = out.astype(o_ref.dtype)


def _mla_sample(page_table, q_lat, q_rope, w_uk_t, w_uv, rows_new, cache, *, n_heads, s_new):
    n, r, kl = q_lat.shape
    n_pages = page_table.shape[1]
    pps = _pages_per_step(n_pages)
    return pl.pallas_call(
        functools.partial(_mla_sample_kernel, pps=pps, n_heads=n_heads, s_new=s_new, kv_lora=kl),
        grid_spec=pltpu.PrefetchScalarGridSpec(
            num_scalar_prefetch=1,
            grid=(n, n_pages // pps),
            in_specs=[_per_seq(q_lat.shape), _per_seq(q_rope.shape), _shared(w_uk_t.shape),
                      _shared(w_uv.shape), _per_seq(rows_new.shape)]
            + _page_specs(kl + MLA_ROPE, n_pages, pps, False),
            out_specs=_per_seq((n, r, MLA_V)),
            scratch_shapes=[pltpu.VMEM((w_uk_t.shape[0] + r, kl), bf16),
                            pltpu.VMEM((r, 1), f32), pltpu.VMEM((r, 1), f32),
                            pltpu.VMEM((r, kl), f32)]),
        out_shape=jax.ShapeDtypeStruct((n, r, MLA_V), bf16),
        compiler_params=_params("parallel", "arbitrary"),
    )(page_table, q_lat, q_rope, w_uk_t, w_uv, rows_new, *([cache] * pps))


def _fox_sample_kernel(pt_ref, q_ref, fn_ref, fnl_ref, new_ref, *rest, pps, group, s_new):
    kv_pages = rest[:pps]
    lf_pages = rest[pps:2 * pps]
    o_ref, k_s, v_s, m_ref, l_ref, acc_ref, carry_ref = rest[2 * pps:]
    c = pl.program_id(1)
    d = HEAD_DIM
    kvw = FOX_KV_HEADS * d
    n_heads = FOX_KV_HEADS * group
    rows = s_new * group

    @pl.when(c == 0)
    def _():
        _softmax_init(m_ref, l_ref, acc_ref)
        carry_ref[...] = jnp.zeros_like(carry_ref)

    later = jnp.where(lax.broadcasted_iota(jnp.int32, (PAGE_SIZE, PAGE_SIZE), 0)
                      > lax.broadcasted_iota(jnp.int32, (PAGE_SIZE, PAGE_SIZE), 1), 1.0, 0.0).astype(bf16)
    parts = [part for t in range(pps) for part in _split3(lf_pages[t][...])]
    sums = _dot(jnp.concatenate(parts, axis=0), later)
    carry = carry_ref[...]
    after = []
    for t in range(pps):
        suffix = (sums[(3 * t) * n_heads:(3 * t + 1) * n_heads]
                  + sums[(3 * t + 1) * n_heads:(3 * t + 2) * n_heads]
                  + sums[(3 * t + 2) * n_heads:(3 * t + 3) * n_heads])
        after.append(suffix + carry)
        carry = carry + (suffix[:, 0:1] + lf_pages[t][:, 0:1])
    carry_ref[...] = carry
    for kh in range(FOX_KV_HEADS):
        for t in range(pps):
            rows_t = slice(t * PAGE_SIZE, (t + 1) * PAGE_SIZE)
            k_s[kh, rows_t, :] = _kv_rows(kv_pages[t], kh, FOX_KV_HEADS).astype(bf16)
            v_s[kh, rows_t, :] = _kv_rows(kv_pages[t], FOX_KV_HEADS + kh, FOX_KV_HEADS).astype(bf16)
        bias = jnp.concatenate(
            [jnp.concatenate([after[t][kh * group:(kh + 1) * group]] * s_new, axis=0)
             for t in range(pps)], axis=1)
        s = _dot_nt(q_ref[0, kh], k_s[kh]) + (bias + fn_ref[0, kh])
        _softmax_step(s, v_s[kh], m_ref, l_ref, acc_ref, (kh,))

    @pl.when(c == pl.num_programs(1) - 1)
    def _():
        mask = _new_key_mask(rows, group, s_new)
        for kh in range(FOX_KV_HEADS):
            kb = new_ref[0, :, kh * d:(kh + 1) * d].astype(bf16)
            vb = new_ref[0, :, kvw + kh * d:kvw + (kh + 1) * d].astype(bf16)
            s = _dot_nt(q_ref[0, kh], kb) + (fn_ref[0, kh] - fnl_ref[0, kh])
            _softmax_step(jnp.where(mask, s, NEG_INF), vb, m_ref, l_ref, acc_ref, (kh,))
            o_ref[0, kh] = (acc_ref[kh] / l_ref[kh]).astype(o_ref.dtype)


def _fox_sample(page_table, q, fn, fnl, kv_new, cache_kv, cache_logf, *, group, s_new):
    n, kh, r, d = q.shape
    n_pages = page_table.shape[1]
    pps = _pages_per_step(n_pages)
    n_heads = kh * group
    return pl.pallas_call(
        functools.partial(_fox_sample_kernel, pps=pps, group=group, s_new=s_new),
        grid_spec=pltpu.PrefetchScalarGridSpec(
            num_scalar_prefetch=1,
            grid=(n, n_pages // pps),
            in_specs=[_per_seq(q.shape), _per_seq(fn.shape), _per_seq(fnl.shape),
                      _per_seq(kv_new.shape)]
            + _page_specs(PAGE_SIZE * 2 * kh, n_pages, pps, True)
            + _page_specs(n_heads, n_pages, pps, True),
            out_specs=_per_seq(q.shape),
            scratch_shapes=[pltpu.VMEM((kh, pps * PAGE_SIZE, d), bf16),
                            pltpu.VMEM((kh, pps * PAGE_SIZE, d), bf16),
                            pltpu.VMEM((kh, r, 1), f32), pltpu.VMEM((kh, r, 1), f32),
                            pltpu.VMEM((kh, r, d), f32), pltpu.VMEM((n_heads, LANES), f32)]),
        out_shape=jax.ShapeDtypeStruct(q.shape, bf16),
        compiler_params=_params("parallel", "arbitrary"),
    )(page_table, q, fn, fnl, kv_new, *([cache_kv] * pps), *([cache_logf] * pps))


def _t5_bucket(dist):
    n = jnp.maximum(dist, 0)
    exact = RPE_BUCKETS // 2
    nf = jnp.maximum(n, 1).astype(f32)
    log_b = exact + (jnp.log(nf / exact) / math.log(RPE_MAX_DIST / exact)
                     * (RPE_BUCKETS - exact)).astype(jnp.int32)
    return jnp.where(n < exact, n, jnp.minimum(log_b, RPE_BUCKETS - 1))


def _toeplitz(g):
    n = (g.shape[0] + 1) // 2
    u = jnp.concatenate([g[n - 1::-1], jnp.zeros((1, g.shape[1]), g.dtype), g[:n - 1:-1]], axis=0)
    b = jnp.tile(u, (n, 1))[:n * (2 * n - 1)].reshape(n, 2 * n - 1, -1)[:, :n]
    return b.transpose(2, 0, 1)


def _rope_tables(pos):
    inv = ROPE_THETA ** (-jnp.arange(0, MLA_ROPE, 2, dtype=f32) / MLA_ROPE)
    ang = pos.astype(f32)[:, None] * inv[None, :]
    cos, sin = jnp.cos(ang), jnp.sin(ang)
    z = jnp.zeros((pos.shape[0], LANES - MLA_ROPE), f32)
    return jnp.concatenate([cos, cos, z], axis=1), jnp.concatenate([-sin, sin, z], axis=1)


def _pad_cols(w, width):
    return jnp.pad(w, ((0, 0), (0, width - w.shape[1])))


def _pad_rows(a, rows):
    return jnp.pad(a, ((0, 0), (0, rows - a.shape[1]), (0, 0)))


def _to_kv_rows(a, n, s, kvh, group):
    return a.reshape(n, s, kvh, group, HEAD_DIM).transpose(0, 2, 1, 3, 4).reshape(
        n, kvh, s * group, HEAD_DIM)


def _from_kv_rows(a, n, s, kvh, group):
    return a.reshape(n, kvh, s, group, HEAD_DIM).transpose(0, 2, 1, 3, 4).reshape(
        n * s, kvh * group * HEAD_DIM)


def kernel(x_prompt, x_sample, c_prompt, c_sample, cache_moba_kv, cache_mla, cache_fox_kv,
           cache_fox_logf, page_table, rpe_table, norm_mix_g, norm_ffn_g, w_ada, b_ada,
           w_in_even, moba_q_g, moba_k_g, mla_q_a_g, w_uq, mla_kv_a_g, w_ukv, mla_q_g, mla_k_g,
           w_out_even, w_in_odd, fox_f_b, fox_q_g, fox_k_g, w_out_odd, w_gate_up, w_down):
    nb_, t, dm = x_prompt.shape
    n, s_new, _ = x_sample.shape
    n_pages = page_table.shape[1]
    past = n_pages * PAGE_SIZE
    assert t % MOBA_BLOCK == 0 and n_pages % 2 == 0 and MOBA_BLOCK >= RPE_MAX_DIST
    moba_heads = dm // (2 * HEAD_DIM)
    mla_heads = dm // (2 * HEAD_DIM)
    fox_heads = dm // HEAD_DIM
    q_lora, kv_lora = dm // 4, dm // 8
    moba_group = moba_heads // MOBA_KV_HEADS
    fox_group = fox_heads // FOX_KV_HEADS
    n_moba_blocks = past // MOBA_BLOCK
    tm = 512

    w_in_even_p = _pad_cols(w_in_even[0], w_in_even.shape[2] + LANES - MLA_ROPE).astype(bf16)
    w_uq_p = jnp.pad(w_uq[0].reshape(q_lora, mla_heads, MLA_QK),
                     ((0, 0), (0, 0), (0, 2 * LANES - MLA_QK))).reshape(q_lora, -1).astype(bf16)
    w_ukv_b = w_ukv[0].astype(bf16)
    w_ukv_h = w_ukv[0].reshape(kv_lora, mla_heads, MLA_NOPE + MLA_V)
    w_uk_t = w_ukv_h[:, :, :MLA_NOPE].reshape(kv_lora, -1).T.astype(bf16)
    w_uv = w_ukv_h[:, :, MLA_NOPE:].reshape(kv_lora, -1).astype(bf16)
    w_in_odd_p = _pad_cols(w_in_odd[0], w_in_odd.shape[2] + LANES - fox_heads).astype(bf16)
    fox_b_p = _pad_cols(fox_f_b[0][None], LANES)
    qk_gain = (jnp.concatenate([mla_q_g[0], mla_q_g[0][MLA_NOPE:]])
               * jnp.concatenate([mla_k_g[0], mla_k_g[0][MLA_NOPE:]]) * MLA_SCALE)
    qk_gain = _pad_cols(qk_gain[None], 2 * LANES)
    even_gains = (moba_q_g[0][None], moba_k_g[0][None], mla_q_a_g[0][None], mla_kv_a_g[0][None])

    offsets = jnp.arange(-(MOBA_BLOCK - 1), MOBA_BLOCK)
    b_own = _toeplitz(rpe_table[_t5_bucket(offsets)])
    b_prev = _toeplitz(rpe_table[_t5_bucket(offsets + MOBA_BLOCK)])
    far = rpe_table[_t5_bucket(jnp.array(MOBA_BLOCK + 1))]
    b_far = jnp.broadcast_to(far[:, None, None], (moba_heads, 1, MOBA_BLOCK))
    s_idx = jnp.arange(s_new)
    chunk_keys = _pages_per_step(n_pages) * PAGE_SIZE
    assert chunk_keys % MOBA_BLOCK == 0
    d_last = (past + s_idx)[:, None] - (past - chunk_keys + jnp.arange(chunk_keys))[None, :]
    sb_last = rpe_table[_t5_bucket(d_last)].transpose(2, 0, 1)
    d_own = s_idx[:, None] - jnp.arange(LANES)[None, :]
    sb_own = rpe_table[_t5_bucket(d_own)].transpose(2, 0, 1)
    rows_sg = lambda b: b.reshape(MOBA_KV_HEADS, moba_group, s_new, -1).transpose(0, 2, 1, 3).reshape(
        MOBA_KV_HEADS, s_new * moba_group, -1)
    sb_last, sb_own = rows_sg(sb_last), rows_sg(sb_own)
    sb_far = jnp.broadcast_to(rows_sg(jnp.broadcast_to(far[:, None, None], (moba_heads, s_new, 1))),
                              (MOBA_KV_HEADS, s_new * moba_group, chunk_keys))

    cos_p, sin_p = _rope_tables(jnp.arange(t))
    cos_s, sin_s = _rope_tables(past + jnp.arange(s_new))
    cos_s, sin_s = jnp.tile(cos_s, (n, 1)), jnp.tile(sin_s, (n, 1))

    caches = dict(
        moba=cache_moba_kv.reshape(cache_moba_kv.shape[0], -1, PAGE_SIZE * 4, HEAD_DIM),
        mla=cache_mla.transpose(0, 1, 3, 2),
        fox=cache_fox_kv.reshape(cache_fox_kv.shape[0], -1, PAGE_SIZE * 4, HEAD_DIM),
        logf=cache_fox_logf.transpose(0, 1, 3, 2))

    c_all = jnp.concatenate([c_prompt, c_sample], axis=0)

    def modulation(layer):
        mod = _matmul(c_all, w_ada[layer].astype(bf16), out_dtype=f32, tm=c_all.shape[0], tn=1024,
                      bias=b_ada[layer][None], silu_in=True)
        parts = jnp.split(mod, 6, axis=-1)
        prompt = [p[:nb_].reshape(nb_, 1, dm) for p in parts]
        sample = [jnp.repeat(p[nb_:], s_new, axis=0).reshape(1, n * s_new, dm) for p in parts]
        return prompt, sample

    def ffn(x, layer, sc, sh, gate):
        h = _norm_mod(x, norm_ffn_g[layer][None], sc, sh, tm)
        a = _matmul(h, w_gate_up[layer].astype(bf16), out_dtype=bf16, tm=1024, tn=512, swiglu=True)
        return _matmul(a, w_down[layer].astype(bf16), out_dtype=f32, tm=1024, tn=512,
                       res=x, gate=gate)

    def even_front(x, sc, sh, cos_t, sin_t, with_kmean):
        h = _norm_mod(x, norm_mix_g[0][None], sc, sh, tm)
        z = _matmul(h, w_in_even_p, out_dtype=f32, tm=tm, tn=w_in_even_p.shape[1])
        qa, kva, kb, vb, cqn, rows, ckvn, kpe, *kmean = _even_post(
            z, cos_t, sin_t, even_gains, tm=tm, n_heads=moba_heads, q_lora=q_lora, kv_lora=kv_lora,
            with_kmean=with_kmean)
        qb = _mla_q(cqn, w_uq_p, cos_t, sin_t, qk_gain, tm=min(2048, cos_t.shape[0]),
                    n_heads=mla_heads)
        return qa, kva, kb, vb, rows, ckvn, kpe, kmean, qb

    def odd_front(x, sc, sh, group_rows):
        h = _norm_mod(x, norm_mix_g[1][None], sc, sh, tm)
        z = _matmul(h, w_in_odd_p, out_dtype=f32, tm=tm, tn=w_in_odd_p.shape[1])
        return _odd_post(z, fox_q_g[0][None], fox_k_g[0][None], fox_b_p, tm=tm, n_heads=fox_heads,
                         group=group_rows)

    (sh1, sc1, g1, sh2, sc2, g2), mods_s0 = modulation(0)
    x = x_prompt.reshape(nb_ * t, dm)
    qa, kva, kb, vb, rows, ckvn, kpe, kmean, qb = even_front(x, sc1, sh1, cos_p, sin_p, True)
    seq = lambda a: a.reshape(nb_, t, -1)
    o_a = _moba_prompt(seq(qa), seq(kb), seq(vb), kmean[0].reshape(nb_, t // MOBA_BLOCK, -1),
                       b_own, b_prev, b_far)
    k_b, v_b = _mla_kv(ckvn, kpe, w_ukv_b, tm=2048, n_heads=mla_heads)
    o_b = _mla_prompt(seq(qb), seq(k_b), seq(v_b), n_heads=mla_heads, tq=512)
    o = jnp.concatenate([o_a, o_b], axis=-1).reshape(nb_ * t, -1)
    x = _matmul(o, w_out_even[0].astype(bf16), out_dtype=f32, tm=1024, tn=512, res=x, gate=g1)
    x = ffn(x, 0, sc2, sh2, g2)
    new_moba_p = kva.reshape(1, nb_, t, 2, MOBA_KV_HEADS, HEAD_DIM)
    new_mla_p = rows.reshape(1, nb_, t, -1)

    (sh1, sc1, g1, sh2, sc2, g2), mods_s1 = modulation(1)
    q, kv, kb, vb, logf, cum = odd_front(x, sc1, sh1, t)
    tq = min(512, t)
    cum_h = cum.reshape(nb_, t, FOX_KV_HEADS, fox_group)
    fq = cum_h.transpose(0, 2, 1, 3)
    fk = cum_h.reshape(nb_, t // tq, tq, FOX_KV_HEADS, fox_group).transpose(0, 3, 1, 4, 2)
    o = _fox_prompt(seq(q), seq(kb), seq(vb), fq, fk, tq=tq).reshape(nb_ * t, -1)
    x = _matmul(o, w_out_odd[0].astype(bf16), out_dtype=f32, tm=1024, tn=512, res=x, gate=g1)
    y_prompt = ffn(x, 1, sc2, sh2, g2).reshape(nb_, t, dm)
    new_fox_p = kv.reshape(1, nb_, t, 2, FOX_KV_HEADS, HEAD_DIM)
    new_logf_p = logf.reshape(1, nb_, t, fox_heads)

    sh1, sc1, g1, sh2, sc2, g2 = mods_s0
    x = x_sample.reshape(n * s_new, dm)
    qa, kva, kb, vb, rows, ckvn, kpe, _, qb = even_front(x, sc1, sh1, cos_s, sin_s, False)
    qa_r = _to_kv_rows(qa, n, s_new, MOBA_KV_HEADS, moba_group)
    kva_pad = _pad_rows(kva.reshape(n, s_new, -1), PAGE_SIZE)
    o_a = _moba_sample(page_table, qa_r, sb_last, sb_far, sb_own, kva_pad, caches["moba"],
                       group=moba_group, s_new=s_new)
    o_a = _from_kv_rows(o_a, n, s_new, MOBA_KV_HEADS, moba_group)
    q_lat = _mla_absorb(qb, w_uk_t, mla_heads).reshape(n, s_new * mla_heads, kv_lora)
    q_rope = qb.reshape(n, s_new * mla_heads, 2 * LANES)[:, :, MLA_NOPE:MLA_QK]
    rows_pad = _pad_rows(rows.reshape(n, s_new, -1), PAGE_SIZE).transpose(0, 2, 1)
    o_b = _mla_sample(page_table, q_lat, q_rope, w_uk_t, w_uv, rows_pad, caches["mla"],
                      n_heads=mla_heads, s_new=s_new).reshape(n * s_new, -1)
    o = jnp.concatenate([o_a, o_b], axis=-1)
    x = _matmul(o, w_out_even[0].astype(bf16), out_dtype=f32, tm=1024, tn=512, res=x, gate=g1)
    x = ffn(x, 0, sc2, sh2, g2)
    new_moba_s = kva.reshape(1, n, s_new, 2, MOBA_KV_HEADS, HEAD_DIM)
    new_mla_s = rows.reshape(1, n, s_new, -1)

    sh1, sc1, g1, sh2, sc2, g2 = mods_s1
    q, kv, kb, vb, logf, cum = odd_front(x, sc1, sh1, s_new)
    q_r = _to_kv_rows(q, n, s_new, FOX_KV_HEADS, fox_group)
    cum_s = cum.reshape(n, s_new, FOX_KV_HEADS, fox_group)
    fn = cum_s.transpose(0, 2, 1, 3).reshape(n, FOX_KV_HEADS, s_new * fox_group, 1)
    fnl = jnp.broadcast_to(cum_s.transpose(0, 2, 3, 1)[:, :, None],
                           (n, FOX_KV_HEADS, s_new, fox_group, s_new)).reshape(
        n, FOX_KV_HEADS, s_new * fox_group, s_new)
    fnl = jnp.pad(fnl, ((0, 0), (0, 0), (0, 0), (0, LANES - s_new)))
    kv_pad = _pad_rows(kv.reshape(n, s_new, -1), PAGE_SIZE)
    o = _fox_sample(page_table, q_r, fn, fnl, kv_pad, caches["fox"], caches["logf"],
                    group=fox_group, s_new=s_new)
    o = _from_kv_rows(o, n, s_new, FOX_KV_HEADS, fox_group)
    x = _matmul(o, w_out_odd[0].astype(bf16), out_dtype=f32, tm=1024, tn=512, res=x, gate=g1)
    y_sample = ffn(x, 1, sc2, sh2, g2).reshape(n, s_new, dm)
    new_fox_s = kv.reshape(1, n, s_new, 2, FOX_KV_HEADS, HEAD_DIM)
    new_logf_s = logf.reshape(1, n, s_new, fox_heads)

    return (y_prompt, y_sample, new_moba_p, new_moba_s, new_mla_p, new_mla_s,
            new_fox_p, new_fox_s, new_logf_p, new_logf_s)
```

```python
import functools
import math

import numpy as np
import jax
import jax.numpy as jnp
from jax import lax
from jax.experimental import pallas as pl
from jax.experimental.pallas import tpu as pltpu

f32 = jnp.float32
bf16 = jnp.bfloat16

HEAD_DIM = 128
MOBA_KV_HEADS = 2
MOBA_BLOCK = 256
MOBA_TOPK = 3
MLA_NOPE = 128
MLA_ROPE = 64
MLA_V = 128
MLA_QK = MLA_NOPE + MLA_ROPE
ROPE_THETA = 10000.0
FOX_KV_HEADS = 2
RPE_BUCKETS = 32
RPE_MAX_DIST = 128
PAGE_SIZE = 128
EPS = 1e-6
NEG_INF = -1e30
ATTN_SCALE = HEAD_DIM ** -0.5
MLA_SCALE = MLA_QK ** -0.5

LANES = 128
VMEM_LIMIT = 56 * 1024 * 1024
PAGES_PER_STEP = 32
STREAM_PAGES_PER_STEP = 64
MLA_KEY_STEP = 512

_NT = (((1,), (1,)), ((), ()))


def _dot(a, b):
    return jnp.dot(a, b, preferred_element_type=f32)


def _dot_nt(a, b):
    return lax.dot_general(a, b, _NT, preferred_element_type=f32)


def _split3(x):
    hi = x.astype(bf16)
    r1 = x - hi.astype(f32)
    mid = r1.astype(bf16)
    lo = (r1 - mid.astype(f32)).astype(bf16)
    return hi, mid, lo


def _rms(x, g):
    return x * lax.rsqrt(jnp.mean(x * x, axis=-1, keepdims=True) + EPS) * g


def _params(*sem):
    return pltpu.CompilerParams(dimension_semantics=sem, vmem_limit_bytes=VMEM_LIMIT)


def _mm_kernel(*refs, nk, swiglu, has_bias, has_res, silu_in):
    it = iter(refs)
    a_ref = next(it)
    w_ref = next(it)
    w2_ref = next(it) if swiglu else None
    b_ref = next(it) if has_bias else None
    x_ref = next(it) if has_res else None
    g_ref = next(it) if has_res else None
    o_ref = next(it)
    accs = list(it)

    a = a_ref[...]
    if silu_in:
        a = a * jax.nn.sigmoid(a)
    a = a.astype(bf16)

    def finish(acc, acc2):
        r = acc
        if swiglu:
            r = (acc * jax.nn.sigmoid(acc)) * acc2
        if has_bias:
            r = r + b_ref[...]
        if has_res:
            r = x_ref[...] + g_ref[0] * r
        o_ref[...] = r.astype(o_ref.dtype)

    if nk == 1:
        finish(_dot(a, w_ref[...]), _dot(a, w2_ref[...]) if swiglu else None)
        return

    k = pl.program_id(2)

    @pl.when(k == 0)
    def _():
        for acc in accs:
            acc[...] = jnp.zeros_like(acc)

    accs[0][...] += _dot(a, w_ref[...])
    if swiglu:
        accs[1][...] += _dot(a, w2_ref[...])

    @pl.when(k == nk - 1)
    def _():
        finish(accs[0][...], accs[1][...] if swiglu else None)


def _matmul(a, w, *, out_dtype, tm, tn, tk=None, swiglu=False, bias=None, res=None,
            gate=None, silu_in=False):
    m, kd = a.shape
    n = w.shape[1] // 2 if swiglu else w.shape[1]
    tm = min(tm, m)
    tn = min(tn, n)
    tk = kd if tk is None else tk
    assert m % tm == 0 and n % tn == 0 and kd % tk == 0
    nk = kd // tk
    nj = n // tn
    in_specs = [pl.BlockSpec((tm, tk), lambda i, j, k: (i, k)),
                pl.BlockSpec((tk, tn), lambda i, j, k: (k, j))]
    args = [a, w]
    if swiglu:
        in_specs.append(pl.BlockSpec((tk, tn), lambda i, j, k: (k, j + nj)))
        args.append(w)
    if bias is not None:
        in_specs.append(pl.BlockSpec((1, tn), lambda i, j, k: (0, j)))
        args.append(bias)
    if res is not None:
        tiles_per_group = (m // tm) // gate.shape[0]
        in_specs.append(pl.BlockSpec((tm, tn), lambda i, j, k: (i, j)))
        in_specs.append(pl.BlockSpec((1, gate.shape[1], tn),
                                     lambda i, j, k: (i // tiles_per_group, 0, j)))
        args += [res, gate]
    scratch = []
    if nk > 1:
        scratch = [pltpu.VMEM((tm, tn), f32)] * (2 if swiglu else 1)
    return pl.pallas_call(
        functools.partial(_mm_kernel, nk=nk, swiglu=swiglu, has_bias=bias is not None,
                          has_res=res is not None, silu_in=silu_in),
        grid=(m // tm, nj, nk),
        in_specs=in_specs,
        out_specs=pl.BlockSpec((tm, tn), lambda i, j, k: (i, j)),
        out_shape=jax.ShapeDtypeStruct((m, n), out_dtype),
        scratch_shapes=scratch,
        compiler_params=_params("parallel", "parallel", "arbitrary"),
    )(*args)


def _norm_mod_kernel(x_ref, g_ref, sc_ref, sh_ref, o_ref):
    y = _rms(x_ref[...], g_ref[...])
    o_ref[...] = (y * (1.0 + sc_ref[0]) + sh_ref[0]).astype(o_ref.dtype)


def _norm_mod(x, g, sc, sh, tm):
    m, d = x.shape
    tm = min(tm, m)
    tiles_per_group = (m // tm) // sc.shape[0]
    mod_spec = pl.BlockSpec((1, sc.shape[1], d), lambda i: (i // tiles_per_group, 0, 0))
    return pl.pallas_call(
        _norm_mod_kernel,
        grid=(m // tm,),
        in_specs=[pl.BlockSpec((tm, d), lambda i: (i, 0)),
                  pl.BlockSpec((1, d), lambda i: (0, 0)), mod_spec, mod_spec],
        out_specs=pl.BlockSpec((tm, d), lambda i: (i, 0)),
        out_shape=jax.ShapeDtypeStruct((m, d), bf16),
        compiler_params=_params("parallel"),
    )(x, g, sc, sh)


def _rope128(x, cos_ref, sin_ref):
    lane = lax.broadcasted_iota(jnp.int32, x.shape, 1)
    swapped = jnp.where(lane < MLA_ROPE // 2, pltpu.roll(x, LANES - MLA_ROPE // 2, 1),
                        pltpu.roll(x, MLA_ROPE // 2, 1))
    return x * cos_ref[...] + swapped * sin_ref[...]


def _store_with_ones(vb_ref, v, n_heads):
    d = HEAD_DIM
    for h in range(n_heads):
        vb_ref[:, 2 * h * d:(2 * h + 1) * d] = v[:, h * d:(h + 1) * d].astype(bf16)
        vb_ref[:, (2 * h + 1) * d:(2 * h + 2) * d] = jnp.ones((v.shape[0], d), bf16)


def _even_post_kernel(z_ref, cos_ref, sin_ref, qg_ref, kg_ref, cqg_ref, ckvg_ref,
                      qa_ref, kva_ref, kb_ref, vb_ref, cqn_ref, rows_ref, ckvn_ref, kpe_ref,
                      *maybe_kmean_ref, n_heads, q_lora, kv_lora):
    tm = z_ref.shape[0]
    d = HEAD_DIM
    for h in range(n_heads):
        qa_ref[:, h * d:(h + 1) * d] = _rms(z_ref[:, h * d:(h + 1) * d], qg_ref[...])
    off = n_heads * d
    for h in range(MOBA_KV_HEADS):
        k = _rms(z_ref[:, off + h * d:off + (h + 1) * d], kg_ref[...])
        kva_ref[:, h * d:(h + 1) * d] = k
        kb_ref[:, h * d:(h + 1) * d] = k.astype(bf16)
        for blk in range(tm // MOBA_BLOCK if maybe_kmean_ref else 0):
            maybe_kmean_ref[0][blk, :, h * d:(h + 1) * d] = jnp.mean(
                k[blk * MOBA_BLOCK:(blk + 1) * MOBA_BLOCK], axis=0, keepdims=True)
    off += MOBA_KV_HEADS * d
    v = z_ref[:, off:off + MOBA_KV_HEADS * d]
    kva_ref[:, MOBA_KV_HEADS * d:] = v
    _store_with_ones(vb_ref, v, MOBA_KV_HEADS)
    off += MOBA_KV_HEADS * d
    cqn_ref[...] = _rms(z_ref[:, off:off + q_lora], cqg_ref[...]).astype(bf16)
    off += q_lora
    ckvn = _rms(z_ref[:, off:off + kv_lora], ckvg_ref[...])
    rows_ref[:, :kv_lora] = ckvn
    ckvn_ref[...] = ckvn.astype(bf16)
    off += kv_lora
    kr = _rope128(z_ref[:, off:off + LANES], cos_ref, sin_ref)
    kpe_ref[...] = kr
    rows_ref[:, kv_lora:] = kr[:, :MLA_ROPE]


def _even_post(z, cos_t, sin_t, prm_g, *, tm, n_heads, q_lora, kv_lora, with_kmean):
    m = z.shape[0]
    tm = min(tm, m)
    t_tiles = cos_t.shape[0] // tm
    d = HEAD_DIM
    row = lambda w: pl.BlockSpec((tm, w), lambda i: (i, 0))
    vec = lambda w: pl.BlockSpec((1, w), lambda i: (0, 0))
    tab = pl.BlockSpec((tm, LANES), lambda i: (i % t_tiles, 0))
    nblk = tm // MOBA_BLOCK
    kvw = MOBA_KV_HEADS * d
    outs = [
        jax.ShapeDtypeStruct((m, n_heads * d), f32),
        jax.ShapeDtypeStruct((m, 2 * kvw), f32),
        jax.ShapeDtypeStruct((m, kvw), bf16),
        jax.ShapeDtypeStruct((m, 2 * kvw), bf16),
        jax.ShapeDtypeStruct((m, q_lora), bf16),
        jax.ShapeDtypeStruct((m, kv_lora + MLA_ROPE), f32),
        jax.ShapeDtypeStruct((m, kv_lora), bf16),
        jax.ShapeDtypeStruct((m, LANES), f32),
    ]
    out_specs = [row(n_heads * d), row(2 * kvw), row(kvw), row(2 * kvw), row(q_lora),
                 row(kv_lora + MLA_ROPE), row(kv_lora), row(LANES)]
    if with_kmean:
        assert tm % MOBA_BLOCK == 0
        outs.append(jax.ShapeDtypeStruct((m // MOBA_BLOCK, 1, kvw), f32))
        out_specs.append(pl.BlockSpec((nblk, 1, kvw), lambda i: (i, 0, 0)))
    return pl.pallas_call(
        functools.partial(_even_post_kernel, n_heads=n_heads, q_lora=q_lora, kv_lora=kv_lora),
        grid=(m // tm,),
        in_specs=[row(z.shape[1]), tab, tab, vec(d), vec(d), vec(q_lora), vec(kv_lora)],
        out_specs=out_specs,
        out_shape=outs,
        compiler_params=_params("parallel"),
    )(z, cos_t, sin_t, *prm_g)


def _mla_q_kernel(c_ref, w_ref, cos_ref, sin_ref, g_ref, o_ref):
    acc = _dot(c_ref[...], w_ref[...])
    nope = acc[:, :MLA_NOPE]
    rope = _rope128(acc[:, MLA_NOPE:], cos_ref, sin_ref)
    ss = jnp.sum(nope * nope, axis=-1, keepdims=True) + jnp.sum(rope * rope, axis=-1, keepdims=True)
    inv = lax.rsqrt(ss * (1.0 / MLA_QK) + EPS)
    o_ref[:, :MLA_NOPE] = (nope * inv * g_ref[:, :MLA_NOPE]).astype(o_ref.dtype)
    o_ref[:, MLA_NOPE:] = (rope * inv * g_ref[:, MLA_NOPE:]).astype(o_ref.dtype)


def _mla_q(cqn, w_uq_pad, cos_t, sin_t, gain, *, tm, n_heads):
    m, ql = cqn.shape
    tm = min(tm, m)
    t_tiles = cos_t.shape[0] // tm
    tab = pl.BlockSpec((tm, LANES), lambda i, h: (i % t_tiles, 0))
    return pl.pallas_call(
        _mla_q_kernel,
        grid=(m // tm, n_heads),
        in_specs=[pl.BlockSpec((tm, ql), lambda i, h: (i, 0)),
                  pl.BlockSpec((ql, 2 * LANES), lambda i, h: (0, h)), tab, tab,
                  pl.BlockSpec((1, 2 * LANES), lambda i, h: (0, 0))],
        out_specs=pl.BlockSpec((tm, 2 * LANES), lambda i, h: (i, h)),
        out_shape=jax.ShapeDtypeStruct((m, n_heads * 2 * LANES), bf16),
        compiler_params=_params("parallel", "arbitrary"),
    )(cqn, w_uq_pad, cos_t, sin_t, gain)


def _mla_kv_kernel(c_ref, kpe_ref, w_ref, k_ref, v_ref):
    acc = _dot(c_ref[...], w_ref[...])
    kn = acc[:, :MLA_NOPE]
    kp = kpe_ref[...]
    ss = jnp.sum(kn * kn, axis=-1, keepdims=True) + jnp.sum(kp * kp, axis=-1, keepdims=True)
    inv = lax.rsqrt(ss * (1.0 / MLA_QK) + EPS)
    k_ref[:, :MLA_NOPE] = (kn * inv).astype(bf16)
    k_ref[:, MLA_NOPE:] = (kp * inv).astype(bf16)
    v_ref[:, :MLA_V] = acc[:, MLA_NOPE:].astype(bf16)
    v_ref[:, MLA_V:] = jnp.ones((acc.shape[0], LANES), bf16)


def _mla_kv(ckvn, kpe, w_ukv, *, tm, n_heads):
    m, kl = ckvn.shape
    tm = min(tm, m)
    return pl.pallas_call(
        _mla_kv_kernel,
        grid=(m // tm, n_heads),
        in_specs=[pl.BlockSpec((tm, kl), lambda i, h: (i, 0)),
                  pl.BlockSpec((tm, LANES), lambda i, h: (i, 0)),
                  pl.BlockSpec((kl, 2 * LANES), lambda i, h: (0, h))],
        out_specs=[pl.BlockSpec((tm, 2 * LANES), lambda i, h: (i, h)),
                   pl.BlockSpec((tm, MLA_V + LANES), lambda i, h: (i, h))],
        out_shape=[jax.ShapeDtypeStruct((m, n_heads * 2 * LANES), bf16),
                   jax.ShapeDtypeStruct((m, n_heads * (MLA_V + LANES)), bf16)],
        compiler_params=_params("parallel", "arbitrary"),
    )(ckvn, kpe, w_ukv)


def _odd_post_kernel(z_ref, qg_ref, kg_ref, fb_ref, q_ref, kv_ref, kb_ref, vb_ref,
                     logf_ref, cum_ref, carry_ref, *, n_heads, group, tiles_per_seq):
    tm = z_ref.shape[0]
    d = HEAD_DIM
    for h in range(n_heads):
        q_ref[:, h * d:(h + 1) * d] = (_rms(z_ref[:, h * d:(h + 1) * d], qg_ref[...])
                                       * ATTN_SCALE).astype(q_ref.dtype)
    off = n_heads * d
    for h in range(FOX_KV_HEADS):
        k = _rms(z_ref[:, off + h * d:off + (h + 1) * d], kg_ref[...])
        kv_ref[:, h * d:(h + 1) * d] = k
        kb_ref[:, h * d:(h + 1) * d] = k.astype(bf16)
    off += FOX_KV_HEADS * d
    v = z_ref[:, off:off + FOX_KV_HEADS * d]
    kv_ref[:, FOX_KV_HEADS * d:] = v
    _store_with_ones(vb_ref, v, FOX_KV_HEADS)
    off += FOX_KV_HEADS * d
    x = z_ref[:, off:off + LANES] + fb_ref[...]
    logf = jnp.minimum(x, 0.0) - jnp.log1p(jnp.exp(-jnp.abs(x)))
    logf_ref[...] = logf[:, :n_heads]

    r = lax.broadcasted_iota(jnp.int32, (tm, tm), 0)
    c = lax.broadcasted_iota(jnp.int32, (tm, tm), 1)
    tri = (c <= r) if group >= tm else ((c <= r) & (c // group == r // group))
    tri = jnp.where(tri, 1.0, 0.0).astype(bf16)
    hi, mid, lo = _split3(logf)
    cum = _dot(tri, hi) + _dot(tri, mid) + _dot(tri, lo)
    if group >= tm:
        first = pl.program_id(0) % tiles_per_seq == 0

        @pl.when(first)
        def _():
            carry_ref[...] = jnp.zeros_like(carry_ref)

        cum = cum + carry_ref[...]
        carry_ref[...] = cum[tm - 1:tm, :]
    cum_ref[...] = cum[:, :n_heads]


def _odd_post(z, qg, kg, fb_pad, *, tm, n_heads, group):
    m = z.shape[0]
    tm = min(tm, m)
    assert group >= tm and group % tm == 0 or tm % group == 0
    d = HEAD_DIM
    kvw = FOX_KV_HEADS * d
    row = lambda w: pl.BlockSpec((tm, w), lambda i: (i, 0))
    vec = lambda w: pl.BlockSpec((1, w), lambda i: (0, 0))
    outs = [jax.ShapeDtypeStruct((m, n_heads * d), bf16),
            jax.ShapeDtypeStruct((m, 2 * kvw), f32),
            jax.ShapeDtypeStruct((m, kvw), bf16),
            jax.ShapeDtypeStruct((m, 2 * kvw), bf16),
            jax.ShapeDtypeStruct((m, n_heads), f32),
            jax.ShapeDtypeStruct((m, n_heads), f32)]
    return pl.pallas_call(
        functools.partial(_odd_post_kernel, n_heads=n_heads, group=group,
                          tiles_per_seq=max(group // tm, 1)),
        grid=(m // tm,),
        in_specs=[row(z.shape[1]), vec(d), vec(d), vec(LANES)],
        out_specs=[row(n_heads * d), row(2 * kvw), row(kvw), row(2 * kvw), row(n_heads), row(n_heads)],
        out_shape=outs,
        scratch_shapes=[pltpu.VMEM((1, LANES), f32)],
        compiler_params=_params("arbitrary"),
    )(z, qg, kg, fb_pad)


def _softmax_init(m_ref, l_ref, acc_ref):
    m_ref[...] = jnp.full_like(m_ref, NEG_INF)
    if l_ref is not None:
        l_ref[...] = jnp.zeros_like(l_ref)
    acc_ref[...] = jnp.zeros_like(acc_ref)


def _normalised(acc, dv):
    return acc[:, :dv] / acc[:, dv:dv + LANES][:, :dv]


def _softmax_step(s, v, m_ref, l_ref, acc_ref, idx=(), nt=False):
    idx = idx if idx else Ellipsis
    m_prev = m_ref[idx]
    m_new = jnp.maximum(m_prev, jnp.max(s, axis=-1, keepdims=True))
    alpha = jnp.exp(m_prev - m_new)
    p = jnp.exp(s - m_new)
    if l_ref is not None:
        l_ref[idx] = alpha * l_ref[idx] + jnp.sum(p, axis=-1, keepdims=True)
    pb = p.astype(bf16)
    if isinstance(v, (list, tuple)):
        w = s.shape[1] // len(v)
        pv = _dot(pb[:, :w], v[0])
        for t in range(1, len(v)):
            pv = pv + _dot(pb[:, t * w:(t + 1) * w], v[t])
    else:
        pv = _dot_nt(pb, v) if nt else _dot(pb, v)
    acc_ref[idx] = alpha * acc_ref[idx] + pv
    m_ref[idx] = m_new


def _with_ones(v):
    return jnp.concatenate([v, jnp.ones((v.shape[0], LANES), v.dtype)], axis=1)


def _causal(tq, tk, q0, k0):
    r = lax.broadcasted_iota(jnp.int32, (tq, tk), 0) + q0
    c = lax.broadcasted_iota(jnp.int32, (tq, tk), 1) + k0
    return r >= c


def _mla_prompt_kernel(q_ref, k_ref, v_ref, o_ref, m_ref, acc_ref, *, tq):
    i = pl.program_id(2)
    q = q_ref[0]
    _softmax_init(m_ref, None, acc_ref)

    def body(j, carry):
        ks = pl.ds(pl.multiple_of(j * tq, tq), tq)
        _softmax_step(_dot_nt(q, k_ref[0, ks, :]), v_ref[0, ks, :], m_ref, None, acc_ref)
        return carry

    lax.fori_loop(0, i, body, 0)
    ks = pl.ds(pl.multiple_of(i * tq, tq), tq)
    s = jnp.where(_causal(tq, tq, 0, 0), _dot_nt(q, k_ref[0, ks, :]), NEG_INF)
    _softmax_step(s, v_ref[0, ks, :], m_ref, None, acc_ref)
    o_ref[0] = _normalised(acc_ref[...], MLA_V).astype(o_ref.dtype)


def _mla_prompt(q, k, v, *, n_heads, tq):
    b, t, _ = q.shape
    tq = min(tq, t)
    return pl.pallas_call(
        functools.partial(_mla_prompt_kernel, tq=tq),
        grid=(b, n_heads, t // tq),
        in_specs=[pl.BlockSpec((1, tq, 2 * LANES), lambda b_, h, i: (b_, i, h)),
                  pl.BlockSpec((1, t, 2 * LANES), lambda b_, h, i: (b_, 0, h)),
                  pl.BlockSpec((1, t, MLA_V + LANES), lambda b_, h, i: (b_, 0, h))],
        out_specs=pl.BlockSpec((1, tq, MLA_V), lambda b_, h, i: (b_, i, h)),
        out_shape=jax.ShapeDtypeStruct((b, t, n_heads * MLA_V), bf16),
        scratch_shapes=[pltpu.VMEM((tq, 1), f32), pltpu.VMEM((tq, MLA_V + LANES), f32)],
        compiler_params=_params("parallel", "parallel", "arbitrary"),
    )(q, k, v)


def _fox_prompt_kernel(q_ref, k_ref, v_ref, fq_ref, fk_ref, o_ref, m_ref, acc_ref,
                       *, tq, group):
    i = pl.program_id(2)
    d = HEAD_DIM
    _softmax_init(m_ref, None, acc_ref)

    def block(j, diag):
        ks = pl.ds(pl.multiple_of(j * tq, tq), tq)
        kb = k_ref[0, ks, :]
        vb = v_ref[0, ks, :]
        for g in range(group):
            s = _dot_nt(q_ref[0, :, g * d:(g + 1) * d], kb)
            s = s + (fq_ref[0, 0, :, g:g + 1] - fk_ref[0, 0, j, g:g + 1, :])
            if diag:
                s = jnp.where(_causal(tq, tq, 0, 0), s, NEG_INF)
            _softmax_step(s, vb, m_ref, None, acc_ref, (g,))

    def body(j, carry):
        block(j, False)
        return carry

    lax.fori_loop(0, i, body, 0)
    block(i, True)
    for g in range(group):
        o_ref[0, :, g * d:(g + 1) * d] = _normalised(acc_ref[g], d).astype(o_ref.dtype)


def _fox_prompt(q, kb, vb, fq, fk, *, tq):
    b, t, hd = q.shape
    group = hd // HEAD_DIM // FOX_KV_HEADS
    assert fk.shape[-1] == tq
    gd = group * HEAD_DIM
    return pl.pallas_call(
        functools.partial(_fox_prompt_kernel, tq=tq, group=group),
        grid=(b, FOX_KV_HEADS, t // tq),
        in_specs=[pl.BlockSpec((1, tq, gd), lambda b_, h, i: (b_, i, h)),
                  pl.BlockSpec((1, t, HEAD_DIM), lambda b_, h, i: (b_, 0, h)),
                  pl.BlockSpec((1, t, 2 * HEAD_DIM), lambda b_, h, i: (b_, 0, h)),
                  pl.BlockSpec((1, 1, tq, group), lambda b_, h, i: (b_, h, i, 0)),
                  pl.BlockSpec((1, 1, t // tq, group, tq), lambda b_, h, i: (b_, h, 0, 0, 0))],
        out_specs=pl.BlockSpec((1, tq, gd), lambda b_, h, i: (b_, i, h)),
        out_shape=jax.ShapeDtypeStruct((b, t, hd), bf16),
        scratch_shapes=[pltpu.VMEM((group, tq, 1), f32),
                        pltpu.VMEM((group, tq, 2 * HEAD_DIM), f32)],
        compiler_params=_params("parallel", "parallel", "arbitrary"),
    )(q, kb, vb, fq, fk)


def _topk_mask(score, n_valid):
    nb = score.shape[1]
    blk = lax.broadcasted_iota(jnp.int32, score.shape, 1)
    score = jnp.where(blk < n_valid, score, NEG_INF)
    rank = jnp.zeros(score.shape, f32)
    for c in range(nb):
        col = score[:, c:c + 1]
        ahead = jnp.where(col > score, 1.0, jnp.where((col == score) & (blk > c), 1.0, 0.0))
        rank = rank + ahead
    return jnp.where((rank < MOBA_TOPK) & (blk < n_valid), 1.0, 0.0)


def _block_scores(q, km):
    qh, qm, ql = _split3(q)
    kh, kmid, kl = _split3(km)
    return (_dot_nt(qh, kh) + (_dot_nt(qh, kmid) + _dot_nt(qm, kh))
            + (_dot_nt(qh, kl) + _dot_nt(qm, kmid) + _dot_nt(ql, kh)))


def _moba_prompt_kernel(q_ref, k_ref, v_ref, km_ref, bown_ref, bprev_ref, bfar_ref, o_ref,
                        sel_ref, qs_ref, m_ref, acc_ref, *, group):
    i = pl.program_id(2)
    d = HEAD_DIM
    tq = MOBA_BLOCK
    nb = km_ref.shape[1]
    _softmax_init(m_ref, None, acc_ref)
    km = km_ref[0]
    for g in range(group):
        q = q_ref[0, :, g * d:(g + 1) * d]
        sel_ref[g] = _topk_mask(_block_scores(q, km), i)
        qs_ref[g] = (q * ATTN_SCALE).astype(bf16)

    def block(j, kind):
        ks = pl.ds(pl.multiple_of(j * tq, tq), tq)
        kb = k_ref[0, ks, :]
        vb = v_ref[0, ks, :]
        for g in range(group):
            s = _dot_nt(qs_ref[g], kb)
            if kind == "own":
                s = jnp.where(_causal(tq, tq, 0, 0), s + bown_ref[g], NEG_INF)
            else:
                s = s + (bprev_ref[g] if kind == "prev" else bfar_ref[g])
                lane = lax.broadcasted_iota(jnp.int32, (tq, nb), 1)
                picked = jnp.sum(jnp.where(lane == j, sel_ref[g], 0.0), axis=-1, keepdims=True)
                s = jnp.where(picked > 0.5, s, NEG_INF)
            _softmax_step(s, vb, m_ref, None, acc_ref, (g,))

    block(i, "own")

    @pl.when(i >= 1)
    def _():
        block(i - 1, "prev")

    def body(j, carry):
        block(j, "far")
        return carry

    lax.fori_loop(0, i - 1, body, 0)
    for g in range(group):
        o_ref[0, :, g * d:(g + 1) * d] = _normalised(acc_ref[g], d).astype(o_ref.dtype)


def _moba_prompt(qa, kb, vb, kmean, b_own, b_prev, b_far):
    b, t, hd = qa.shape
    group = hd // HEAD_DIM // MOBA_KV_HEADS
    nb = t // MOBA_BLOCK
    gd = group * HEAD_DIM
    blk = MOBA_BLOCK
    bias_spec = pl.BlockSpec((group, blk, blk), lambda b_, h, i: (h, 0, 0))
    return pl.pallas_call(
        functools.partial(_moba_prompt_kernel, group=group),
        grid=(b, MOBA_KV_HEADS, nb),
        in_specs=[pl.BlockSpec((1, blk, gd), lambda b_, h, i: (b_, i, h)),
                  pl.BlockSpec((1, t, HEAD_DIM), lambda b_, h, i: (b_, 0, h)),
                  pl.BlockSpec((1, t, 2 * HEAD_DIM), lambda b_, h, i: (b_, 0, h)),
                  pl.BlockSpec((1, nb, HEAD_DIM), lambda b_, h, i: (b_, 0, h)),
                  bias_spec, bias_spec,
                  pl.BlockSpec((group, 1, blk), lambda b_, h, i: (h, 0, 0))],
        out_specs=pl.BlockSpec((1, blk, gd), lambda b_, h, i: (b_, i, h)),
        out_shape=jax.ShapeDtypeStruct((b, t, hd), bf16),
        scratch_shapes=[pltpu.VMEM((group, blk, nb), f32),
                        pltpu.VMEM((group, blk, HEAD_DIM), bf16),
                        pltpu.VMEM((group, blk, 1), f32),
                        pltpu.VMEM((group, blk, 2 * HEAD_DIM), f32)],
        compiler_params=_params("parallel", "parallel", "arbitrary"),
    )(qa, kb, vb, kmean, b_own, b_prev, b_far)


def _pages_per_step(n_pages, want=PAGES_PER_STEP):
    return want if n_pages % want == 0 else n_pages


def _page_specs(rows, n_pages, pps, reverse):
    def spec(t):
        def index(n, c, pt):
            p = c * pps + t
            return (0, pt[n, n_pages - 1 - p if reverse else p], 0, 0)
        return pl.BlockSpec((None, None, rows, LANES), index)
    return [spec(t) for t in range(pps)]


def _kv_rows(page_ref, which, n_kv_heads):
    return page_ref[pl.ds(which, PAGE_SIZE, stride=2 * n_kv_heads), :]


def _per_seq(shape):
    nd = len(shape)
    return pl.BlockSpec((1,) + tuple(shape[1:]), lambda n, c, pt: (n,) + (0,) * (nd - 1))


def _shared(shape):
    nd = len(shape)
    return pl.BlockSpec(tuple(shape), lambda n, c, pt: (0,) * nd)


def _new_key_mask(rows, group, s_new):
    r = lax.broadcasted_iota(jnp.int32, (rows, LANES), 0) // group
    j = lax.broadcasted_iota(jnp.int32, (rows, LANES), 1)
    return (j <= r) & (j < s_new)


def _moba_sample_kernel(pt_ref, q_ref, blast_ref, bfar_ref, bown_ref, new_ref, *rest,
                        pps, group, s_new):
    pages = rest[:pps]
    o_ref, k_s, v_s, km_s, sel_s, m_ref, l_ref, acc_ref = rest[pps:]
    c = pl.program_id(1)
    nc = pl.num_programs(1)
    d = HEAD_DIM
    kvh = MOBA_KV_HEADS
    rows, nb = sel_s.shape[1], sel_s.shape[2]
    chunk_keys = pps * PAGE_SIZE
    blocks_per_chunk = chunk_keys // MOBA_BLOCK

    for kh in range(kvh):
        sums = []
        for t in range(pps):
            k = _kv_rows(pages[t], kh, kvh)
            v = _kv_rows(pages[t], kvh + kh, kvh)
            r0 = pl.multiple_of((c * pps + t) * PAGE_SIZE, PAGE_SIZE)
            k_s[kh, pl.ds(r0, PAGE_SIZE), :] = k.astype(bf16)
            v_s[kh, pl.ds(r0, PAGE_SIZE), :] = v.astype(bf16)
            sums.append(jnp.sum(k, axis=0, keepdims=True))
        for t in range(0, pps, 2):
            km_s[kh, pl.ds(c * blocks_per_chunk + t // 2, 1), :] = (
                (sums[t] + sums[t + 1]) * (1.0 / MOBA_BLOCK))

    @pl.when(c == nc - 1)
    def _():
        _softmax_init(m_ref, l_ref, acc_ref)
        lane = lax.broadcasted_iota(jnp.int32, (rows, nb), 1)
        for kh in range(kvh):
            sel_s[kh] = _topk_mask(_block_scores(q_ref[0, kh], km_s[kh]), nb)
        qs = [(q_ref[0, kh] * ATTN_SCALE).astype(bf16) for kh in range(kvh)]

        def chunk(ci, carry):
            ks = pl.ds(pl.multiple_of(ci * chunk_keys, chunk_keys), chunk_keys)
            for kh in range(kvh):
                s = _dot_nt(qs[kh], k_s[kh, ks, :])
                bias = jnp.where(ci == nc - 1, blast_ref[kh], bfar_ref[kh])
                sel = sel_s[kh]
                picked = jnp.concatenate(
                    [jnp.broadcast_to(
                        jnp.sum(jnp.where(lane == ci * blocks_per_chunk + b, sel, 0.0),
                                axis=-1, keepdims=True), (rows, MOBA_BLOCK))
                     for b in range(blocks_per_chunk)], axis=1)
                s = jnp.where(picked > 0.5, s + bias, NEG_INF)
                _softmax_step(s, v_s[kh, ks, :], m_ref, l_ref, acc_ref, (kh,))
            return carry

        lax.fori_loop(0, nc, chunk, 0)
        mask = _new_key_mask(rows, group, s_new)
        for kh in range(kvh):
            kb = new_ref[0, :, kh * d:(kh + 1) * d].astype(bf16)
            vb = new_ref[0, :, (kvh + kh) * d:(kvh + kh + 1) * d].astype(bf16)
            s = jnp.where(mask, _dot_nt(qs[kh], kb) + bown_ref[kh], NEG_INF)
            _softmax_step(s, vb, m_ref, l_ref, acc_ref, (kh,))
            o_ref[0, kh] = (acc_ref[kh] / l_ref[kh]).astype(o_ref.dtype)


def _moba_sample(page_table, q, b_last, b_far, b_own, kv_new, cache, *, group, s_new):
    n, kh, r, d = q.shape
    n_pages = page_table.shape[1]
    pps = _pages_per_step(n_pages)
    past = n_pages * PAGE_SIZE
    nb = past // MOBA_BLOCK
    assert pps % 2 == 0
    return pl.pallas_call(
        functools.partial(_moba_sample_kernel, pps=pps, group=group, s_new=s_new),
        grid_spec=pltpu.PrefetchScalarGridSpec(
            num_scalar_prefetch=1,
            grid=(n, n_pages // pps),
            in_specs=[_per_seq(q.shape), _shared(b_last.shape), _shared(b_far.shape),
                      _shared(b_own.shape), _per_seq(kv_new.shape)]
            + _page_specs(PAGE_SIZE * 2 * kh, n_pages, pps, False),
            out_specs=_per_seq(q.shape),
            scratch_shapes=[pltpu.VMEM((kh, past, d), bf16), pltpu.VMEM((kh, past, d), bf16),
                            pltpu.VMEM((kh, nb, d), f32), pltpu.VMEM((kh, r, nb), f32),
                            pltpu.VMEM((kh, r, 1), f32), pltpu.VMEM((kh, r, 1), f32),
                            pltpu.VMEM((kh, r, d), f32)]),
        out_shape=jax.ShapeDtypeStruct(q.shape, bf16),
        compiler_params=_params("arbitrary", "arbitrary"),
    )(page_table, q, b_last, b_far, b_own, kv_new, *([cache] * pps))


def _mla_absorb_kernel(q_ref, w_ref, o_ref):
    o_ref[...] = _dot(q_ref[...], w_ref[...]).astype(o_ref.dtype)


def _mla_absorb(qb, w_uk_t, n_heads):
    m = qb.shape[0]
    kl = w_uk_t.shape[1]
    return pl.pallas_call(
        _mla_absorb_kernel,
        grid=(n_heads,),
        in_specs=[pl.BlockSpec((m, MLA_NOPE), lambda h: (0, 2 * h)),
                  pl.BlockSpec((MLA_NOPE, kl), lambda h: (h, 0))],
        out_specs=pl.BlockSpec((m, kl), lambda h: (0, h)),
        out_shape=jax.ShapeDtypeStruct((m, n_heads * kl), bf16),
        compiler_params=_params("parallel"),
    )(qb, w_uk_t)


def _mla_sample_kernel(pt_ref, ql_ref, qr_ref, wk_ref, wv_ref, new_ref, *rest,
                       pps, n_heads, s_new, kv_lora):
    pages = rest[:pps]
    o_ref, lhs_s, m_ref, l_ref, acc_ref = rest[pps:]
    c = pl.program_id(1)
    rows = s_new * n_heads
    n_up = n_heads * MLA_NOPE

    @pl.when(c == 0)
    def _():
        _softmax_init(m_ref, l_ref, acc_ref)
        lhs_s[:n_up, :] = wk_ref[...]
        lhs_s[n_up:, :] = ql_ref[0]

    def attend(lat_t, kp_t, mask):
        n_keys = lat_t.shape[1]
        step = min(n_keys, MLA_KEY_STEP)
        ssq, s_lat = [], []
        for k0 in range(0, n_keys, step):
            both = _dot(lhs_s[...], lat_t[:, k0:k0 + step])
            ssq.append(jnp.concatenate(
                [jnp.sum(jnp.square(both[h * MLA_NOPE:(h + 1) * MLA_NOPE]), axis=0, keepdims=True)
                 for h in range(n_heads)], axis=0))
            s_lat.append(both[n_up:])
        ssq = jnp.concatenate(ssq, axis=1) + jnp.sum(kp_t * kp_t, axis=0, keepdims=True)
        inv = lax.rsqrt(ssq * (1.0 / MLA_QK) + EPS)
        s = jnp.concatenate(s_lat, axis=1) + _dot(qr_ref[0], kp_t.astype(bf16))
        s = s * jnp.concatenate([inv] * s_new, axis=0)
        if mask is not None:
            s = jnp.where(mask, s, NEG_INF)
        _softmax_step(s, lat_t, m_ref, l_ref, acc_ref, nt=True)

    attend(jnp.concatenate([p[:kv_lora, :].astype(bf16) for p in pages], axis=1),
           jnp.concatenate([p[kv_lora:, :] for p in pages], axis=1), None)

    @pl.when(c == pl.num_programs(1) - 1)
    def _():
        attend(new_ref[0, :kv_lora, :].astype(bf16), new_ref[0, kv_lora:, :],
               _new_key_mask(rows, n_heads, s_new))
        lat = (acc_ref[...] / l_ref[...]).astype(bf16)
        full = _dot(lat, wv_ref[...])
        head = lax.broadcasted_iota(jnp.int32, (rows, MLA_V), 0) % n_heads
        out = jnp.zeros((rows, MLA_V), f32)
        for h in range(n_heads):
            out = out + jnp.where(head == h, full[:, h * MLA_V:(h + 1) * MLA_V], 0.0)
        o_ref[0] = out.astype(o_ref.dtype)


def _mla_sample(page_table, q_lat, q_rope, w_uk_t, w_uv, rows_new, cache, *, n_heads, s_new):
    n, r, kl = q_lat.shape
    n_pages = page_table.shape[1]
    pps = _pages_per_step(n_pages, STREAM_PAGES_PER_STEP)
    return pl.pallas_call(
        functools.partial(_mla_sample_kernel, pps=pps, n_heads=n_heads, s_new=s_new, kv_lora=kl),
        grid_spec=pltpu.PrefetchScalarGridSpec(
            num_scalar_prefetch=1,
            grid=(n, n_pages // pps),
            in_specs=[_per_seq(q_lat.shape), _per_seq(q_rope.shape), _shared(w_uk_t.shape),
                      _shared(w_uv.shape), _per_seq(rows_new.shape)]
            + _page_specs(kl + MLA_ROPE, n_pages, pps, False),
            out_specs=_per_seq((n, r, MLA_V)),
            scratch_shapes=[pltpu.VMEM((w_uk_t.shape[0] + r, kl), bf16),
                            pltpu.VMEM((r, 1), f32), pltpu.VMEM((r, 1), f32),
                            pltpu.VMEM((r, kl), f32)]),
        out_shape=jax.ShapeDtypeStruct((n, r, MLA_V), bf16),
        compiler_params=_params("parallel", "arbitrary"),
    )(page_table, q_lat, q_rope, w_uk_t, w_uv, rows_new, *([cache] * pps))


def _fox_sample_kernel(pt_ref, q_ref, fn_ref, fnl_ref, new_ref, *rest, pps, group, s_new):
    kv_pages = rest[:pps]
    lf_pages = rest[pps:2 * pps]
    o_ref, k_s, v_s, m_ref, l_ref, acc_ref, carry_ref = rest[2 * pps:]
    c = pl.program_id(1)
    d = HEAD_DIM
    kvw = FOX_KV_HEADS * d
    n_heads = FOX_KV_HEADS * group
    rows = s_new * group

    @pl.when(c == 0)
    def _():
        _softmax_init(m_ref, l_ref, acc_ref)
        carry_ref[...] = jnp.zeros_like(carry_ref)

    later = jnp.where(lax.broadcasted_iota(jnp.int32, (PAGE_SIZE, PAGE_SIZE), 0)
                      > lax.broadcasted_iota(jnp.int32, (PAGE_SIZE, PAGE_SIZE), 1), 1.0, 0.0).astype(bf16)
    parts = [part for t in range(pps) for part in _split3(lf_pages[t][...])]
    sums = _dot(jnp.concatenate(parts, axis=0), later)
    carry = carry_ref[...]
    after = []
    for t in range(pps):
        suffix = (sums[(3 * t) * n_heads:(3 * t + 1) * n_heads]
                  + sums[(3 * t + 1) * n_heads:(3 * t + 2) * n_heads]
                  + sums[(3 * t + 2) * n_heads:(3 * t + 3) * n_heads])
        after.append(suffix + carry)
        carry = carry + (suffix[:, 0:1] + lf_pages[t][:, 0:1])
    carry_ref[...] = carry
    for kh in range(FOX_KV_HEADS):
        for t in range(pps):
            rows_t = slice(t * PAGE_SIZE, (t + 1) * PAGE_SIZE)
            k_s[kh, rows_t, :] = _kv_rows(kv_pages[t], kh, FOX_KV_HEADS).astype(bf16)
            v_s[kh, rows_t, :] = _kv_rows(kv_pages[t], FOX_KV_HEADS + kh, FOX_KV_HEADS).astype(bf16)
        bias = jnp.concatenate(
            [jnp.concatenate([after[t][kh * group:(kh + 1) * group]] * s_new, axis=0)
             for t in range(pps)], axis=1)
        s = _dot_nt(q_ref[0, kh], k_s[kh]) + (bias + fn_ref[0, kh])
        _softmax_step(s, v_s[kh], m_ref, l_ref, acc_ref, (kh,))

    @pl.when(c == pl.num_programs(1) - 1)
    def _():
        mask = _new_key_mask(rows, group, s_new)
        for kh in range(FOX_KV_HEADS):
            kb = new_ref[0, :, kh * d:(kh + 1) * d].astype(bf16)
            vb = new_ref[0, :, kvw + kh * d:kvw + (kh + 1) * d].astype(bf16)
            s = _dot_nt(q_ref[0, kh], kb) + (fn_ref[0, kh] - fnl_ref[0, kh])
            _softmax_step(jnp.where(mask, s, NEG_INF), vb, m_ref, l_ref, acc_ref, (kh,))
            o_ref[0, kh] = (acc_ref[kh] / l_ref[kh]).astype(o_ref.dtype)


def _fox_sample(page_table, q, fn, fnl, kv_new, cache_kv, cache_logf, *, group, s_new):
    n, kh, r, d = q.shape
    n_pages = page_table.shape[1]
    pps = _pages_per_step(n_pages, STREAM_PAGES_PER_STEP)
    n_heads = kh * group
    return pl.pallas_call(
        functools.partial(_fox_sample_kernel, pps=pps, group=group, s_new=s_new),
        grid_spec=pltpu.PrefetchScalarGridSpec(
            num_scalar_prefetch=1,
            grid=(n, n_pages // pps),
            in_specs=[_per_seq(q.shape), _per_seq(fn.shape), _per_seq(fnl.shape),
                      _per_seq(kv_new.shape)]
            + _page_specs(PAGE_SIZE * 2 * kh, n_pages, pps, True)
            + _page_specs(n_heads, n_pages, pps, True),
            out_specs=_per_seq(q.shape),
            scratch_shapes=[pltpu.VMEM((kh, pps * PAGE_SIZE, d), bf16),
                            pltpu.VMEM((kh, pps * PAGE_SIZE, d), bf16),
                            pltpu.VMEM((kh, r, 1), f32), pltpu.VMEM((kh, r, 1), f32),
                            pltpu.VMEM((kh, r, d), f32), pltpu.VMEM((n_heads, LANES), f32)]),
        out_shape=jax.ShapeDtypeStruct(q.shape, bf16),
        compiler_params=_params("parallel", "arbitrary"),
    )(page_table, q, fn, fnl, kv_new, *([cache_kv] * pps), *([cache_logf] * pps))


def _t5_bucket(dist):
    n = jnp.maximum(dist, 0)
    exact = RPE_BUCKETS // 2
    nf = jnp.maximum(n, 1).astype(f32)
    log_b = exact + (jnp.log(nf / exact) / math.log(RPE_MAX_DIST / exact)
                     * (RPE_BUCKETS - exact)).astype(jnp.int32)
    return jnp.where(n < exact, n, jnp.minimum(log_b, RPE_BUCKETS - 1))


def _toeplitz(g):
    n = (g.shape[0] + 1) // 2
    u = jnp.concatenate([g[n - 1::-1], jnp.zeros((1, g.shape[1]), g.dtype), g[:n - 1:-1]], axis=0)
    b = jnp.tile(u, (n, 1))[:n * (2 * n - 1)].reshape(n, 2 * n - 1, -1)[:, :n]
    return b.transpose(2, 0, 1)


def _rope_tables(pos):
    inv = ROPE_THETA ** (-jnp.arange(0, MLA_ROPE, 2, dtype=f32) / MLA_ROPE)
    ang = pos.astype(f32)[:, None] * inv[None, :]
    cos, sin = jnp.cos(ang), jnp.sin(ang)
    z = jnp.zeros((pos.shape[0], LANES - MLA_ROPE), f32)
    return jnp.concatenate([cos, cos, z], axis=1), jnp.concatenate([-sin, sin, z], axis=1)


def _pad_cols(w, width):
    return jnp.pad(w, ((0, 0), (0, width - w.shape[1])))


def _pad_rows(a, rows):
    return jnp.pad(a, ((0, 0), (0, rows - a.shape[1]), (0, 0)))


def _to_kv_rows(a, n, s, kvh, group):
    return a.reshape(n, s, kvh, group, HEAD_DIM).transpose(0, 2, 1, 3, 4).reshape(
        n, kvh, s * group, HEAD_DIM)


def _from_kv_rows(a, n, s, kvh, group):
    return a.reshape(n, kvh, s, group, HEAD_DIM).transpose(0, 2, 1, 3, 4).reshape(
        n * s, kvh * group * HEAD_DIM)


def kernel(x_prompt, x_sample, c_prompt, c_sample, cache_moba_kv, cache_mla, cache_fox_kv,
           cache_fox_logf, page_table, rpe_table, norm_mix_g, norm_ffn_g, w_ada, b_ada,
           w_in_even, moba_q_g, moba_k_g, mla_q_a_g, w_uq, mla_kv_a_g, w_ukv, mla_q_g, mla_k_g,
           w_out_even, w_in_odd, fox_f_b, fox_q_g, fox_k_g, w_out_odd, w_gate_up, w_down):
    nb_, t, dm = x_prompt.shape
    n, s_new, _ = x_sample.shape
    n_pages = page_table.shape[1]
    past = n_pages * PAGE_SIZE
    assert t % MOBA_BLOCK == 0 and n_pages % 2 == 0 and MOBA_BLOCK >= RPE_MAX_DIST
    moba_heads = dm // (2 * HEAD_DIM)
    mla_heads = dm // (2 * HEAD_DIM)
    fox_heads = dm // HEAD_DIM
    q_lora, kv_lora = dm // 4, dm // 8
    moba_group = moba_heads // MOBA_KV_HEADS
    fox_group = fox_heads // FOX_KV_HEADS
    n_moba_blocks = past // MOBA_BLOCK
    tm = 512

    w_in_even_p = _pad_cols(w_in_even[0], w_in_even.shape[2] + LANES - MLA_ROPE).astype(bf16)
    w_uq_p = jnp.pad(w_uq[0].reshape(q_lora, mla_heads, MLA_QK),
                     ((0, 0), (0, 0), (0, 2 * LANES - MLA_QK))).reshape(q_lora, -1).astype(bf16)
    w_ukv_b = w_ukv[0].astype(bf16)
    w_ukv_h = w_ukv[0].reshape(kv_lora, mla_heads, MLA_NOPE + MLA_V)
    w_uk_t = w_ukv_h[:, :, :MLA_NOPE].reshape(kv_lora, -1).T.astype(bf16)
    w_uv = w_ukv_h[:, :, MLA_NOPE:].reshape(kv_lora, -1).astype(bf16)
    w_in_odd_p = _pad_cols(w_in_odd[0], w_in_odd.shape[2] + LANES - fox_heads).astype(bf16)
    fox_b_p = _pad_cols(fox_f_b[0][None], LANES)
    qk_gain = (jnp.concatenate([mla_q_g[0], mla_q_g[0][MLA_NOPE:]])
               * jnp.concatenate([mla_k_g[0], mla_k_g[0][MLA_NOPE:]]) * MLA_SCALE)
    qk_gain = _pad_cols(qk_gain[None], 2 * LANES)
    even_gains = (moba_q_g[0][None], moba_k_g[0][None], mla_q_a_g[0][None], mla_kv_a_g[0][None])

    offsets = jnp.arange(-(MOBA_BLOCK - 1), MOBA_BLOCK)
    b_own = _toeplitz(rpe_table[_t5_bucket(offsets)])
    b_prev = _toeplitz(rpe_table[_t5_bucket(offsets + MOBA_BLOCK)])
    far = rpe_table[_t5_bucket(jnp.array(MOBA_BLOCK + 1))]
    b_far = jnp.broadcast_to(far[:, None, None], (moba_heads, 1, MOBA_BLOCK))
    s_idx = jnp.arange(s_new)
    chunk_keys = _pages_per_step(n_pages) * PAGE_SIZE
    assert chunk_keys % MOBA_BLOCK == 0
    d_last = (past + s_idx)[:, None] - (past - chunk_keys + jnp.arange(chunk_keys))[None, :]
    sb_last = rpe_table[_t5_bucket(d_last)].transpose(2, 0, 1)
    d_own = s_idx[:, None] - jnp.arange(LANES)[None, :]
    sb_own = rpe_table[_t5_bucket(d_own)].transpose(2, 0, 1)
    rows_sg = lambda b: b.reshape(MOBA_KV_HEADS, moba_group, s_new, -1).transpose(0, 2, 1, 3).reshape(
        MOBA_KV_HEADS, s_new * moba_group, -1)
    sb_last, sb_own = rows_sg(sb_last), rows_sg(sb_own)
    sb_far = jnp.broadcast_to(rows_sg(jnp.broadcast_to(far[:, None, None], (moba_heads, s_new, 1))),
                              (MOBA_KV_HEADS, s_new * moba_group, chunk_keys))

    cos_p, sin_p = _rope_tables(jnp.arange(t))
    cos_s, sin_s = _rope_tables(past + jnp.arange(s_new))
    cos_s, sin_s = jnp.tile(cos_s, (n, 1)), jnp.tile(sin_s, (n, 1))

    caches = dict(
        moba=cache_moba_kv.reshape(cache_moba_kv.shape[0], -1, PAGE_SIZE * 4, HEAD_DIM),
        mla=cache_mla.transpose(0, 1, 3, 2),
        fox=cache_fox_kv.reshape(cache_fox_kv.shape[0], -1, PAGE_SIZE * 4, HEAD_DIM),
        logf=cache_fox_logf.transpose(0, 1, 3, 2))

    c_all = jnp.concatenate([c_prompt, c_sample], axis=0)

    def modulation(layer):
        mod = _matmul(c_all, w_ada[layer].astype(bf16), out_dtype=f32, tm=c_all.shape[0], tn=1024,
                      bias=b_ada[layer][None], silu_in=True)
        parts = jnp.split(mod, 6, axis=-1)
        prompt = [p[:nb_].reshape(nb_, 1, dm) for p in parts]
        sample = [jnp.repeat(p[nb_:], s_new, axis=0).reshape(1, n * s_new, dm) for p in parts]
        return prompt, sample

    def ffn(x, layer, sc, sh, gate):
        h = _norm_mod(x, norm_ffn_g[layer][None], sc, sh, tm)
        a = _matmul(h, w_gate_up[layer].astype(bf16), out_dtype=bf16, tm=1024, tn=512, swiglu=True)
        return _matmul(a, w_down[layer].astype(bf16), out_dtype=f32, tm=1024, tn=512,
                       res=x, gate=gate)

    def even_front(x, sc, sh, cos_t, sin_t, with_kmean):
        h = _norm_mod(x, norm_mix_g[0][None], sc, sh, tm)
        z = _matmul(h, w_in_even_p, out_dtype=f32, tm=tm, tn=w_in_even_p.shape[1])
        qa, kva, kb, vb, cqn, rows, ckvn, kpe, *kmean = _even_post(
            z, cos_t, sin_t, even_gains, tm=tm, n_heads=moba_heads, q_lora=q_lora, kv_lora=kv_lora,
            with_kmean=with_kmean)
        qb = _mla_q(cqn, w_uq_p, cos_t, sin_t, qk_gain, tm=min(2048, cos_t.shape[0]),
                    n_heads=mla_heads)
        return qa, kva, kb, vb, rows, ckvn, kpe, kmean, qb

    def odd_front(x, sc, sh, group_rows):
        h = _norm_mod(x, norm_mix_g[1][None], sc, sh, tm)
        z = _matmul(h, w_in_odd_p, out_dtype=f32, tm=tm, tn=w_in_odd_p.shape[1])
        return _odd_post(z, fox_q_g[0][None], fox_k_g[0][None], fox_b_p, tm=tm, n_heads=fox_heads,
                         group=group_rows)

    (sh1, sc1, g1, sh2, sc2, g2), mods_s0 = modulation(0)
    x = x_prompt.reshape(nb_ * t, dm)
    qa, kva, kb, vb, rows, ckvn, kpe, kmean, qb = even_front(x, sc1, sh1, cos_p, sin_p, True)
    seq = lambda a: a.reshape(nb_, t, -1)
    o_a = _moba_prompt(seq(qa), seq(kb), seq(vb), kmean[0].reshape(nb_, t // MOBA_BLOCK, -1),
                       b_own, b_prev, b_far)
    k_b, v_b = _mla_kv(ckvn, kpe, w_ukv_b, tm=2048, n_heads=mla_heads)
    o_b = _mla_prompt(seq(qb), seq(k_b), seq(v_b), n_heads=mla_heads, tq=512)
    o = jnp.concatenate([o_a, o_b], axis=-1).reshape(nb_ * t, -1)
    x = _matmul(o, w_out_even[0].astype(bf16), out_dtype=f32, tm=1024, tn=512, res=x, gate=g1)
    x = ffn(x, 0, sc2, sh2, g2)
    new_moba_p = kva.reshape(1, nb_, t, 2, MOBA_KV_HEADS, HEAD_DIM)
    new_mla_p = rows.reshape(1, nb_, t, -1)

    (sh1, sc1, g1, sh2, sc2, g2), mods_s1 = modulation(1)
    q, kv, kb, vb, logf, cum = odd_front(x, sc1, sh1, t)
    tq = min(512, t)
    cum_h = cum.reshape(nb_, t, FOX_KV_HEADS, fox_group)
    fq = cum_h.transpose(0, 2, 1, 3)
    fk = cum_h.reshape(nb_, t // tq, tq, FOX_KV_HEADS, fox_group).transpose(0, 3, 1, 4, 2)
    o = _fox_prompt(seq(q), seq(kb), seq(vb), fq, fk, tq=tq).reshape(nb_ * t, -1)
    x = _matmul(o, w_out_odd[0].astype(bf16), out_dtype=f32, tm=1024, tn=512, res=x, gate=g1)
    y_prompt = ffn(x, 1, sc2, sh2, g2).reshape(nb_, t, dm)
    new_fox_p = kv.reshape(1, nb_, t, 2, FOX_KV_HEADS, HEAD_DIM)
    new_logf_p = logf.reshape(1, nb_, t, fox_heads)

    sh1, sc1, g1, sh2, sc2, g2 = mods_s0
    x = x_sample.reshape(n * s_new, dm)
    qa, kva, kb, vb, rows, ckvn, kpe, _, qb = even_front(x, sc1, sh1, cos_s, sin_s, False)
    qa_r = _to_kv_rows(qa, n, s_new, MOBA_KV_HEADS, moba_group)
    kva_pad = _pad_rows(kva.reshape(n, s_new, -1), PAGE_SIZE)
    o_a = _moba_sample(page_table, qa_r, sb_last, sb_far, sb_own, kva_pad, caches["moba"],
                       group=moba_group, s_new=s_new)
    o_a = _from_kv_rows(o_a, n, s_new, MOBA_KV_HEADS, moba_group)
    q_lat = _mla_absorb(qb, w_uk_t, mla_heads).reshape(n, s_new * mla_heads, kv_lora)
    q_rope = qb.reshape(n, s_new * mla_heads, 2 * LANES)[:, :, MLA_NOPE:MLA_QK]
    rows_pad = _pad_rows(rows.reshape(n, s_new, -1), PAGE_SIZE).transpose(0, 2, 1)
    o_b = _mla_sample(page_table, q_lat, q_rope, w_uk_t, w_uv, rows_pad, caches["mla"],
                      n_heads=mla_heads, s_new=s_new).reshape(n * s_new, -1)
    o = jnp.concatenate([o_a, o_b], axis=-1)
    x = _matmul(o, w_out_even[0].astype(bf16), out_dtype=f32, tm=1024, tn=512, res=x, gate=g1)
    x = ffn(x, 0, sc2, sh2, g2)
    new_moba_s = kva.reshape(1, n, s_new, 2, MOBA_KV_HEADS, HEAD_DIM)
    new_mla_s = rows.reshape(1, n, s_new, -1)

    sh1, sc1, g1, sh2, sc2, g2 = mods_s1
    q, kv, kb, vb, logf, cum = odd_front(x, sc1, sh1, s_new)
    q_r = _to_kv_rows(q, n, s_new, FOX_KV_HEADS, fox_group)
    cum_s = cum.reshape(n, s_new, FOX_KV_HEADS, fox_group)
    fn = cum_s.transpose(0, 2, 1, 3).reshape(n, FOX_KV_HEADS, s_new * fox_group, 1)
    fnl = jnp.broadcast_to(cum_s.transpose(0, 2, 3, 1)[:, :, None],
                           (n, FOX_KV_HEADS, s_new, fox_group, s_new)).reshape(
        n, FOX_KV_HEADS, s_new * fox_group, s_new)
    fnl = jnp.pad(fnl, ((0, 0), (0, 0), (0, 0), (0, LANES - s_new)))
    kv_pad = _pad_rows(kv.reshape(n, s_new, -1), PAGE_SIZE)
    o = _fox_sample(page_table, q_r, fn, fnl, kv_pad, caches["fox"], caches["logf"],
                    group=fox_group, s_new=s_new)
    o = _from_kv_rows(o, n, s_new, FOX_KV_HEADS, fox_group)
    x = _matmul(o, w_out_odd[0].astype(bf16), out_dtype=f32, tm=1024, tn=512, res=x, gate=g1)
    y_sample = ffn(x, 1, sc2, sh2, g2).reshape(n, s_new, dm)
    new_fox_s = kv.reshape(1, n, s_new, 2, FOX_KV_HEADS, HEAD_DIM)
    new_logf_s = logf.reshape(1, n, s_new, fox_heads)

    return (y_prompt, y_sample, new_moba_p, new_moba_s, new_mla_p, new_mla_s,
            new_fox_p, new_fox_s, new_logf_p, new_logf_s)
```
